```python
import math
import jax
import jax.numpy as jnp
from jax import lax
import numpy as np

D_MODEL = 1024
BATCH = 4
SEQ = 4096
DEPTH = 2
DEC_BATCH = 32
DEC_SEQ = 8
PAST_LEN = 16384
PAGE_SIZE = 128

N_HEADS_A = 8
HEAD_DIM_A = 64
WIDTH_A = N_HEADS_A * HEAD_DIM_A
Q_BLOCK = 128
SB_BIAS_INIT = -6.0
N_HEADS_B = 4
DK_B = 128
DV_B = 128
WIDTH_B = N_HEADS_B * DK_B
VWIDTH_B = N_HEADS_B * DV_B
CHUNK_B = 64
D_FF = 2816
P_DIM = 256
ALPHA = (2.0 * DEPTH) ** 0.25
BETA_INIT = (8.0 * DEPTH) ** -0.25
LN_EPS = 1e-5
RMS_EPS = 1e-6
IN_WIDTHS = (WIDTH_A, WIDTH_A, WIDTH_A, WIDTH_B, WIDTH_B, VWIDTH_B, VWIDTH_B, D_MODEL, D_MODEL)
N_IN = 3 * WIDTH_A + 2 * WIDTH_B + 2 * VWIDTH_B + 2 * D_MODEL

kernel_name = 'stickbreak_hgrn2_macaron_hybrid'


def _split_points():
    pts, acc = [], 0
    for w in IN_WIDTHS[:-1]:
        acc += w
        pts.append(acc)
    return pts


def layer_norm(x, g, b):
    xf = x.astype(jnp.float32)
    mu = jnp.mean(xf, axis=-1, keepdims=True)
    var = jnp.mean(jnp.square(xf - mu), axis=-1, keepdims=True)
    return ((xf - mu) * lax.rsqrt(var + LN_EPS) * g + b).astype(x.dtype)


def swiglu(x, w_g, w_u, w_d):
    return (jax.nn.silu(x @ w_g) * (x @ w_u)) @ w_d


def stick_breaking(q, k, v, k_past, v_past, bias):
    B, L, H, Dh = q.shape
    P = 0 if k_past is None else k_past.shape[1]
    scale = 1.0 / math.sqrt(Dh)
    bias_f = bias.astype(jnp.float32)[None, :, None, None]
    outs = []
    for qs in range(0, L, Q_BLOCK):
        nq = min(Q_BLOCK, L - qs)
        nk = max(qs + nq - 1, 1)
        qb = q[:, qs:qs + nq]
        z = jnp.einsum('bqhd,bkhd->bhqk', qb, k[:, :nk]).astype(jnp.float32)
        if P:
            z_past = jnp.einsum('bqhd,bkhd->bhqk', qb, k_past).astype(jnp.float32)
            z = jnp.concatenate([z_past, z], axis=-1)
        z = z * scale + bias_f
        t_pos = P + qs + jnp.arange(nq)
        s_pos = jnp.arange(P + nk)
        causal = s_pos[None, :] < t_pos[:, None]
        log_1mb = jnp.where(causal, jax.nn.log_sigmoid(-z), 0.0)
        log_stick = lax.cumsum(log_1mb, axis=3, reverse=True) - log_1mb
        a = jnp.where(causal, jnp.exp(jax.nn.log_sigmoid(z) + log_stick), 0.0).astype(v.dtype)
        o = jnp.einsum('bhqk,bkhd->bqhd', a[..., P:], v[:, :nk])
        if P:
            o = o + jnp.einsum('bhqk,bkhd->bqhd', a[..., :P], v_past)
        outs.append(o)
    return jnp.concatenate(outs, axis=1)


def hgrn2_recurrence(q, k, logf, v, s0, chunk):
    B, L, H, DK = q.shape
    n = L // chunk

    def to_chunks(a):
        return a.reshape(B, n, chunk, H, a.shape[-1]).transpose(1, 0, 3, 2, 4)

    mask = jnp.tril(jnp.ones((chunk, chunk), dtype=bool))

    def step(S, inp):
        qc, kc, gc, vc = inp
        b = jnp.cumsum(gc, axis=2)
        o = jnp.einsum('bhtk,bhkv->bhtv', qc * jnp.exp(b), S)
        diff = b[:, :, :, None, :] - b[:, :, None, :, :]
        decay = jnp.exp(jnp.where(mask[:, :, None], diff, -jnp.inf))
        attn = jnp.einsum('bhtk,bhtsk,bhsk->bhts', qc, decay, kc)
        o = o + jnp.einsum('bhts,bhsv->bhtv', attn, vc)
        b_last = b[:, :, -1:]
        S = (jnp.exp(b_last[:, :, 0])[..., None] * S
             + jnp.einsum('bhsk,bhsv->bhkv', kc * jnp.exp(b_last - b), vc))
        return S, o

    S, o = lax.scan(step, s0, (to_chunks(q), to_chunks(k), to_chunks(logf), to_chunks(v)))
    o = o.transpose(1, 0, 3, 2, 4).reshape(B, L, H, v.shape[-1])
    return o, S


def token_mix(h, lb, w_in, b_merge, sb_bias, norm_g, w_oa, w_ob, w_out, k_past, v_past, s0):
    B, L, _ = h.shape
    proj = h @ w_in
    qa, ka, va, qb, fb, ib, gb, za, zb = jnp.split(proj, _split_points(), axis=-1)
    qa = qa.reshape(B, L, N_HEADS_A, HEAD_DIM_A)
    ka = ka.reshape(B, L, N_HEADS_A, HEAD_DIM_A)
    va = va.reshape(B, L, N_HEADS_A, HEAD_DIM_A)
    oa = stick_breaking(qa, ka, va, k_past, v_past, sb_bias).reshape(B, L, WIDTH_A)
    lb_h = lb.reshape(N_HEADS_B, DK_B)
    zf = fb.astype(jnp.float32).reshape(B, L, N_HEADS_B, DK_B)
    logf = jnp.logaddexp(jnp.log(lb_h), jnp.log1p(-lb_h) + jax.nn.log_sigmoid(zf))
    kb = (1.0 - lb_h) * jax.nn.sigmoid(-zf)
    qb = jax.nn.silu(qb.astype(jnp.float32)).reshape(B, L, N_HEADS_B, DK_B)
    vb = ib.astype(jnp.float32).reshape(B, L, N_HEADS_B, DV_B)
    if s0 is None:
        s0 = jnp.zeros((B, N_HEADS_B, DK_B, DV_B), jnp.float32)
    ob, s_new = hgrn2_recurrence(qb, kb, logf, vb, s0.astype(jnp.float32), math.gcd(CHUNK_B, L))
    ob = ob * lax.rsqrt(jnp.mean(jnp.square(ob), axis=-1, keepdims=True) + RMS_EPS) * norm_g
    ob = (ob.reshape(B, L, VWIDTH_B) * jax.nn.silu(gb.astype(jnp.float32))).astype(h.dtype)
    ya = oa @ w_oa
    yb = ob @ w_ob
    m = jax.nn.sigmoid(za + b_merge[0]) * ya + jax.nn.sigmoid(zb + b_merge[1]) * yb
    return m @ w_out, ka, va, s_new.astype(h.dtype)


def decoder_layer(x, p, k_past, v_past, s0, lb, ln_g, ln_b, ffn_wg, ffn_wu, ffn_wd,
                  w_in, b_merge, sb_bias, norm_g, w_oa, w_ob, w_out, pe_w_proj, pe_w_gate,
                  pe_b_gate):
    x = layer_norm(ALPHA * x + 0.5 * swiglu(x, ffn_wg[0], ffn_wu[0], ffn_wd[0]), ln_g[0], ln_b[0])
    mix, k_new, v_new, s_new = token_mix(x, lb, w_in, b_merge, sb_bias, norm_g, w_oa, w_ob,
                                         w_out, k_past, v_past, s0)
    x = layer_norm(ALPHA * x + mix, ln_g[1], ln_b[1])
    x = layer_norm(ALPHA * x + 0.5 * swiglu(x, ffn_wg[1], ffn_wu[1], ffn_wd[1]), ln_g[2], ln_b[2])
    e = jax.nn.sigmoid(x @ pe_w_gate + pe_b_gate) * (p @ pe_w_proj)
    x = layer_norm(ALPHA * x + e, ln_g[3], ln_b[3])
    return x, k_new, v_new, s_new


def setup_inputs(seed: int = 0) -> dict:
    key = jax.random.key(seed)
    ks = jax.random.split(key, 24)
    n_pages = PAST_LEN // PAGE_SIZE
    n_pool = (DEC_BATCH * n_pages * 5) // 4
    nrm = jax.random.normal
    f32 = jnp.float32
    x_prompt = nrm(ks[0], (BATCH, SEQ, D_MODEL), f32)
    x_sample = nrm(ks[1], (DEC_BATCH, DEC_SEQ, D_MODEL), f32)
    cache_k = nrm(ks[2], (DEPTH, n_pool, PAGE_SIZE, N_HEADS_A, HEAD_DIM_A), f32)
    cache_v = nrm(ks[3], (DEPTH, n_pool, PAGE_SIZE, N_HEADS_A, HEAD_DIM_A), f32)
    state_hgrn = 0.5 * nrm(ks[4], (DEPTH, DEC_BATCH, N_HEADS_B, DK_B, DV_B), f32)
    page_table = jax.random.permutation(ks[5], n_pool)[:DEC_BATCH * n_pages]
    page_table = page_table.reshape(DEC_BATCH, n_pages).astype(jnp.int32)
    p_prompt = nrm(ks[6], (DEPTH, BATCH, SEQ, P_DIM), f32)
    p_sample = nrm(ks[7], (DEPTH, DEC_BATCH, DEC_SEQ, P_DIM), f32)
    ln_g = 1.0 + 0.02 * nrm(ks[8], (DEPTH, 4, D_MODEL), f32)
    ln_b = 0.02 * nrm(ks[9], (DEPTH, 4, D_MODEL), f32)
    ffn_w_gate = nrm(ks[10], (DEPTH, 2, D_MODEL, D_FF), f32) * D_MODEL ** -0.5
    ffn_w_up = nrm(ks[11], (DEPTH, 2, D_MODEL, D_FF), f32) * D_MODEL ** -0.5
    ffn_w_down = nrm(ks[12], (DEPTH, 2, D_FF, D_MODEL), f32) * (D_FF ** -0.5 * BETA_INIT)
    w_in = nrm(ks[13], (DEPTH, D_MODEL, N_IN), f32) * D_MODEL ** -0.5
    b_merge = 0.02 * nrm(ks[14], (DEPTH, 2, D_MODEL), f32)
    sb_bias = SB_BIAS_INIT + 0.1 * nrm(ks[23], (DEPTH, N_HEADS_A), f32)
    hgrn_lb = 0.5 * nrm(ks[15], (DEPTH, WIDTH_B), f32)
    hgrn_norm_g = 1.0 + 0.02 * nrm(ks[16], (DEPTH, DV_B), f32)
    w_oa = nrm(ks[17], (DEPTH, WIDTH_A, D_MODEL), f32) * WIDTH_A ** -0.5
    w_ob = nrm(ks[18], (DEPTH, VWIDTH_B, D_MODEL), f32) * VWIDTH_B ** -0.5
    w_out = nrm(ks[19], (DEPTH, D_MODEL, D_MODEL), f32) * (D_MODEL ** -0.5 * BETA_INIT)
    pe_w_proj = nrm(ks[20], (DEPTH, P_DIM, D_MODEL), f32) * (P_DIM ** -0.5 * BETA_INIT)
    pe_w_gate = nrm(ks[21], (DEPTH, D_MODEL, D_MODEL), f32) * D_MODEL ** -0.5
    pe_b_gate = 0.02 * nrm(ks[22], (DEPTH, D_MODEL), f32)
    return {'x_prompt': x_prompt, 'x_sample': x_sample, 'cache_k': cache_k, 'cache_v': cache_v,
            'state_hgrn': state_hgrn, 'page_table': page_table, 'p_prompt': p_prompt,
            'p_sample': p_sample, 'ln_g': ln_g, 'ln_b': ln_b, 'ffn_w_gate': ffn_w_gate,
            'ffn_w_up': ffn_w_up, 'ffn_w_down': ffn_w_down, 'w_in': w_in, 'b_merge': b_merge,
            'sb_bias': sb_bias, 'hgrn_lb': hgrn_lb, 'hgrn_norm_g': hgrn_norm_g, 'w_oa': w_oa,
            'w_ob': w_ob, 'w_out': w_out, 'pe_w_proj': pe_w_proj, 'pe_w_gate': pe_w_gate,
            'pe_b_gate': pe_b_gate}


def reference(x_prompt, x_sample, cache_k, cache_v, state_hgrn, page_table, p_prompt, p_sample,
              ln_g, ln_b, ffn_w_gate, ffn_w_up, ffn_w_down, w_in, b_merge, sb_bias, hgrn_lb,
              hgrn_norm_g, w_oa, w_ob, w_out, pe_w_proj, pe_w_gate, pe_b_gate):
    lb_all = jnp.cumsum(jax.nn.softmax(hgrn_lb.astype(jnp.float32), axis=0), axis=0)
    lb_all = lb_all - lb_all[0:1]
    dec_b, n_pages = page_table.shape
    past = n_pages * cache_k.shape[2]
    y_prompt, y_sample = x_prompt, x_sample
    kp, vp, sp, kd, vd, sd = [], [], [], [], [], []
    for i in range(DEPTH):
        w = (ln_g[i], ln_b[i], ffn_w_gate[i], ffn_w_up[i], ffn_w_down[i], w_in[i], b_merge[i],
             sb_bias[i], hgrn_norm_g[i], w_oa[i], w_ob[i], w_out[i], pe_w_proj[i],
             pe_w_gate[i], pe_b_gate[i])
        y_prompt, k_new, v_new, s_new = decoder_layer(y_prompt, p_prompt[i], None, None, None,
                                                      lb_all[i], *w)
        kp.append(k_new)
        vp.append(v_new)
        sp.append(s_new)
        k_past = cache_k[i][page_table].reshape(dec_b, past, N_HEADS_A, HEAD_DIM_A)
        v_past = cache_v[i][page_table].reshape(dec_b, past, N_HEADS_A, HEAD_DIM_A)
        y_sample, k_new, v_new, s_new = decoder_layer(y_sample, p_sample[i], k_past, v_past,
                                                      state_hgrn[i], lb_all[i], *w)
        kd.append(k_new)
        vd.append(v_new)
        sd.append(s_new)
    return (y_prompt, y_sample, jnp.stack(kp), jnp.stack(vp), jnp.stack(sp),
            jnp.stack(kd), jnp.stack(vd), jnp.stack(sd))
```

```python
import functools
import math

import jax
import jax.numpy as jnp
from jax import lax
from jax.experimental import pallas as pl
from jax.experimental.pallas import tpu as pltpu

F32 = jnp.float32
BF16 = jnp.bfloat16

D_MODEL = 1024
N_HEADS_A = 8
HEAD_DIM_A = 64
WIDTH_A = N_HEADS_A * HEAD_DIM_A
N_HEADS_B = 4
DK_B = 128
DV_B = 128
WIDTH_B = N_HEADS_B * DK_B
D_FF = 2816
P_DIM = 256
LN_EPS = 1e-5
RMS_EPS = 1e-6
COL = 512
N_COLBLK = 11
(CB_QA, CB_KA, CB_VA, CB_QB, CB_FB, CB_IB, CB_GB, CB_ZA, CB_ZB) = (0, 1, 2, 3, 4, 5, 6, 7, 9)

LANES = 128
VMEM_LIMIT_BYTES = 56 * 1024 * 1024
FFN_TILE_F = D_FF // 2
SUBCHUNK_B = 16
PAGES_PER_STEP = 8


def _cparams(sem):
    return pltpu.CompilerParams(dimension_semantics=sem, vmem_limit_bytes=VMEM_LIMIT_BYTES)


def _dot(a, b):
    return jnp.dot(a, b, preferred_element_type=F32)


def _dot_nt(a, b):
    return lax.dot_general(a, b, (((1,), (1,)), ((), ())), preferred_element_type=F32)


def _dot_tn(a, b):
    return lax.dot_general(a, b, (((0,), (0,)), ((), ())), preferred_element_type=F32)


def _split2(x):
    hi = x.astype(BF16)
    lo = (x - hi.astype(F32)).astype(BF16)
    return hi, lo


def _split3(x):
    h1 = x.astype(BF16)
    r = x - h1.astype(F32)
    h2 = r.astype(BF16)
    h3 = (r - h2.astype(F32)).astype(BF16)
    return h1, h2, h3


def _log_sigmoid(z):
    return -(jnp.maximum(-z, 0.0) + jnp.log1p(jnp.exp(-jnp.abs(z))))


def _layer_norm(y, g, b):
    mu = jnp.mean(y, axis=-1, keepdims=True)
    yc = y - mu
    var = jnp.mean(yc * yc, axis=-1, keepdims=True)
    return yc * lax.rsqrt(var + LN_EPS) * g + b


def _ffn_ln_kernel(x_ref, wg_ref, wu_ref, wd_ref, g_ref, b_ref, o_ref, acc_ref, *, alpha):
    j = pl.program_id(1)
    x = x_ref[...]
    xb = x.astype(BF16)
    g = _dot(xb, wg_ref[...])
    u = _dot(xb, wu_ref[...])
    h = (g * jax.nn.sigmoid(g) * u).astype(BF16)
    part = _dot(h, wd_ref[...])

    @pl.when(j == 0)
    def _():
        acc_ref[...] = part

    @pl.when(j > 0)
    def _():
        acc_ref[...] += part

    @pl.when(j == pl.num_programs(1) - 1)
    def _():
        y = alpha * x + 0.5 * acc_ref[...]
        o_ref[...] = _layer_norm(y, g_ref[...], b_ref[...])


def _ffn_ln(x, wg, wu, wd, ln_g, ln_b, layer, which, ln_idx, alpha):
    n = x.shape[0]
    tm = min(512, n)
    tf = FFN_TILE_F
    return pl.pallas_call(
        functools.partial(_ffn_ln_kernel, alpha=alpha),
        grid=(n // tm, D_FF // tf),
        in_specs=[
            pl.BlockSpec((tm, D_MODEL), lambda i, j: (i, 0)),
            pl.BlockSpec((None, None, D_MODEL, tf), lambda i, j: (layer, which, 0, j)),
            pl.BlockSpec((None, None, D_MODEL, tf), lambda i, j: (layer, which, 0, j)),
            pl.BlockSpec((None, None, tf, D_MODEL), lambda i, j: (layer, which, j, 0)),
            pl.BlockSpec((None, None, 1, D_MODEL), lambda i, j: (layer, ln_idx, 0, 0)),
            pl.BlockSpec((None, None, 1, D_MODEL), lambda i, j: (layer, ln_idx, 0, 0)),
        ],
        out_specs=pl.BlockSpec((tm, D_MODEL), lambda i, j: (i, 0)),
        out_shape=jax.ShapeDtypeStruct((n, D_MODEL), F32),
        scratch_shapes=[pltpu.VMEM((tm, D_MODEL), F32)],
        compiler_params=_cparams(("parallel", "arbitrary")),
        name="ffn_ln",
    )(x, wg, wu, wd, ln_g, ln_b)


def _in_proj_kernel(x_ref, w_ref, o_ref):
    o_ref[...] = _dot(x_ref[...].astype(BF16), w_ref[...])


def _in_proj(x, w_in, layer):
    n = x.shape[0]
    tm = min(1024, n)
    return pl.pallas_call(
        _in_proj_kernel,
        grid=(n // tm, N_COLBLK),
        in_specs=[
            pl.BlockSpec((tm, D_MODEL), lambda i, j: (i, 0)),
            pl.BlockSpec((None, D_MODEL, COL), lambda i, j: (layer, 0, j)),
        ],
        out_specs=pl.BlockSpec((None, tm, COL), lambda i, j: (j, i, 0)),
        out_shape=jax.ShapeDtypeStruct((N_COLBLK, n, COL), F32),
        compiler_params=_cparams(("parallel", "arbitrary")),
        name="in_proj",
    )(x, w_in)


def _sb_block(qh, kb, vb, bias, tri, c, o, causal):
    z = _dot_nt(qh, kb) + bias
    l_full = -(jnp.maximum(z, 0.0) + jnp.log1p(jnp.exp(-jnp.abs(z))))
    log_beta = z + l_full
    l = l_full if causal is None else jnp.where(causal, l_full, 0.0)
    hi, lo = _split2(l)
    later = _dot(hi, tri) + _dot(lo, tri)
    a = jnp.exp(log_beta + later + c)
    if causal is not None:
        a = jnp.where(causal, a, 0.0)
    o = o + _dot(a.astype(BF16), vb)
    c = c + jnp.sum(l, axis=-1, keepdims=True)
    return c, o


def _attn_prompt_kernel(bias_ref, q_ref, k_ref, v_ref, o_ref, *, tile):
    pair = pl.program_id(1)
    qi = pl.program_id(2)
    scale = 1.0 / math.sqrt(HEAD_DIM_A)
    q = q_ref[...] * scale
    lane = lax.broadcasted_iota(jnp.int32, (tile, LANES), 1)
    row = lax.broadcasted_iota(jnp.int32, (tile, tile), 0)
    col = lax.broadcasted_iota(jnp.int32, (tile, tile), 1)
    causal = col < row
    tri = jnp.where(row > col, 1.0, 0.0).astype(BF16)
    out = jnp.zeros((tile, LANES), F32)
    for hh in range(2):
        head_lanes = (lane // HEAD_DIM_A) == hh
        qh = jnp.where(head_lanes, q, 0.0).astype(BF16)
        bias = bias_ref[2 * pair + hh]

        def load_kv(kj):
            start = pl.multiple_of(kj * tile, tile)
            kb = k_ref[pl.ds(start, tile), :].astype(BF16)
            vb = jnp.where(head_lanes, v_ref[pl.ds(start, tile), :], 0.0).astype(BF16)
            return kb, vb

        kb, vb = load_kv(qi)
        c0 = jnp.zeros((tile, 1), F32)
        o0 = jnp.zeros((tile, LANES), F32)
        c, o = _sb_block(qh, kb, vb, bias, tri, c0, o0, causal)

        def body(jj, carry):
            c, o = carry
            kb, vb = load_kv(qi - 1 - jj)
            return _sb_block(qh, kb, vb, bias, tri, c, o, None)

        c, o = lax.fori_loop(0, qi, body, (c, o))
        out = out + o
    o_ref[...] = out


def _attn_prompt(proj3, sb_bias, batch, seq):
    tile = min(256, seq)
    nq = seq // tile
    n = batch * seq
    return pl.pallas_call(
        functools.partial(_attn_prompt_kernel, tile=tile),
        grid=(batch, WIDTH_A // LANES, nq),
        in_specs=[
            pl.BlockSpec(memory_space=pltpu.SMEM),
            pl.BlockSpec((None, tile, LANES), lambda b, p, i: (CB_QA, b * nq + i, p)),
            pl.BlockSpec((None, seq, LANES), lambda b, p, i: (CB_KA, b, p)),
            pl.BlockSpec((None, seq, LANES), lambda b, p, i: (CB_VA, b, p)),
        ],
        out_specs=pl.BlockSpec((tile, LANES), lambda b, p, i: (b * nq + i, p)),
        out_shape=jax.ShapeDtypeStruct((n, WIDTH_A), F32),
        compiler_params=_cparams(("parallel", "parallel", "arbitrary")),
        name="sb_attn_prompt",
    )(sb_bias, proj3, proj3, proj3)


def _attn_decode_kernel(pt_ref, bias_ref, q_ref, kn_ref, vn_ref, *rest, n_new, page, pps):
    k_refs = rest[:pps]
    v_refs = rest[pps:2 * pps]
    o_ref = rest[2 * pps]
    qbd_scr, c_scr, acc_scr, kpad_scr, vpad_scr = rest[2 * pps + 1:]
    g = pl.program_id(1)
    rows = N_HEADS_A * n_new
    row = lax.broadcasted_iota(jnp.int32, (page, page), 0)
    col = lax.broadcasted_iota(jnp.int32, (page, page), 1)
    tri = jnp.where(row > col, 1.0, 0.0).astype(BF16)
    bias = bias_ref[...]

    def block(kb, vb, causal):
        z = _dot_nt(qbd_scr[...], kb) + bias
        l_full = -(jnp.maximum(z, 0.0) + jnp.log1p(jnp.exp(-jnp.abs(z))))
        log_beta = z + l_full
        l = l_full if causal is None else jnp.where(causal, l_full, 0.0)
        hi, lo = _split2(l)
        later = _dot(hi, tri) + _dot(lo, tri)
        a = jnp.exp(log_beta + later + c_scr[...])
        if causal is not None:
            a = jnp.where(causal, a, 0.0)
        acc_scr[...] += _dot(a.astype(BF16), vb)
        c_scr[...] += jnp.sum(l, axis=-1, keepdims=True)

    @pl.when(g == 0)
    def _():
        scale = 1.0 / math.sqrt(HEAD_DIM_A)
        q = q_ref[...] * scale
        qt = jnp.concatenate([q] * N_HEADS_A, axis=0)
        r = lax.broadcasted_iota(jnp.int32, (rows, WIDTH_A), 0)
        l_ = lax.broadcasted_iota(jnp.int32, (rows, WIDTH_A), 1)
        qbd_scr[...] = jnp.where(r // n_new == l_ // HEAD_DIM_A, qt, 0.0).astype(BF16)
        c_scr[...] = jnp.zeros_like(c_scr)
        acc_scr[...] = jnp.zeros_like(acc_scr)
        kpad_scr[...] = jnp.zeros_like(kpad_scr)
        vpad_scr[...] = jnp.zeros_like(vpad_scr)
        kpad_scr[0:n_new, :] = kn_ref[...]
        vpad_scr[0:n_new, :] = vn_ref[...]
        rr = lax.broadcasted_iota(jnp.int32, (rows, page), 0)
        ss = lax.broadcasted_iota(jnp.int32, (rows, page), 1)
        causal = ss < (rr % n_new)
        block(kpad_scr[...].astype(BF16), vpad_scr[...].astype(BF16), causal)

    for i in range(pps):
        block(k_refs[i][...].astype(BF16), v_refs[i][...].astype(BF16), None)

    @pl.when(g == pl.num_programs(1) - 1)
    def _():
        acc = acc_scr[...]
        lane = lax.broadcasted_iota(jnp.int32, (n_new, WIDTH_A), 1)
        out = jnp.zeros((n_new, WIDTH_A), F32)
        for h in range(N_HEADS_A):
            out = out + jnp.where(lane // HEAD_DIM_A == h, acc[h * n_new:(h + 1) * n_new, :], 0.0)
        o_ref[...] = out


def _attn_decode(proj3, cache_k, cache_v, page_table, sb_bias, layer, dec_b, n_new):
    page = cache_k.shape[2]
    n_pages = page_table.shape[1]
    pps = math.gcd(PAGES_PER_STEP, n_pages)
    steps = n_pages // pps
    rows = N_HEADS_A * n_new
    bias_rows = jnp.repeat(sb_bias.astype(F32), n_new)[:, None]

    def page_spec(i):
        return pl.BlockSpec(
            (None, None, page, WIDTH_A),
            lambda b, g, pt: (layer, pt[b, n_pages - 1 - (g * pps + i)], 0, 0))

    grid_spec = pltpu.PrefetchScalarGridSpec(
        num_scalar_prefetch=1,
        grid=(dec_b, steps),
        in_specs=[
            pl.BlockSpec((rows, 1), lambda b, g, pt: (0, 0)),
            pl.BlockSpec((None, n_new, COL), lambda b, g, pt: (CB_QA, b, 0)),
            pl.BlockSpec((None, n_new, COL), lambda b, g, pt: (CB_KA, b, 0)),
            pl.BlockSpec((None, n_new, COL), lambda b, g, pt: (CB_VA, b, 0)),
        ] + [page_spec(i) for i in range(pps)] + [page_spec(i) for i in range(pps)],
        out_specs=pl.BlockSpec((n_new, WIDTH_A), lambda b, g, pt: (b, 0)),
        scratch_shapes=[
            pltpu.VMEM((rows, WIDTH_A), BF16),
            pltpu.VMEM((rows, 1), F32),
            pltpu.VMEM((rows, WIDTH_A), F32),
            pltpu.VMEM((page, WIDTH_A), F32),
            pltpu.VMEM((page, WIDTH_A), F32),
        ],
    )
    return pl.pallas_call(
        functools.partial(_attn_decode_kernel, n_new=n_new, page=page, pps=pps),
        grid_spec=grid_spec,
        out_shape=jax.ShapeDtypeStruct((dec_b * n_new, WIDTH_A), F32),
        compiler_params=_cparams(("parallel", "arbitrary")),
        name="sb_attn_decode",
    )(page_table, bias_rows, proj3, proj3, proj3, *([cache_k] * pps), *([cache_v] * pps))


def _hgrn_kernel(lb_ref, ng_ref, q_ref, f_ref, i_ref, g_ref, s0_ref, o_ref, s_ref,
                 st_scr, b_scr, q_scr, k_scr, o_scr, *, chunk, sub):
    c = pl.program_id(1)

    @pl.when(c == 0)
    def _():
        st_scr[...] = s0_ref[...]

    lb = lb_ref[...]
    zf = f_ref[...]
    a1 = jnp.log(lb)
    a2 = jnp.log1p(-lb) + _log_sigmoid(zf)
    logf = jnp.maximum(a1, a2) + jnp.log1p(jnp.exp(-jnp.abs(a1 - a2)))
    k_scr[...] = (1.0 - lb) * jax.nn.sigmoid(-zf)
    qv = q_ref[...]
    q_scr[...] = qv * jax.nn.sigmoid(qv)
    r = lax.broadcasted_iota(jnp.int32, (chunk, chunk), 0)
    s = lax.broadcasted_iota(jnp.int32, (chunk, chunk), 1)
    lmat = jnp.where((s <= r) & (s // sub == r // sub), 1.0, 0.0).astype(BF16)
    h1, h2, h3 = _split3(logf)
    b_scr[...] = _dot(lmat, h1) + _dot(lmat, h2) + _dot(lmat, h3)

    tidx = lax.broadcasted_iota(jnp.int32, (sub, 1), 0)

    def sub_step(ci, carry):
        r0 = pl.multiple_of(ci * sub, sub)
        for h in range(N_HEADS_B):
            cols = slice(h * DK_B, (h + 1) * DK_B)
            b = b_scr[pl.ds(r0, sub), cols]
            qs = q_scr[pl.ds(r0, sub), cols]
            kk = k_scr[pl.ds(r0, sub), cols]
            vv = i_ref[pl.ds(r0, sub), cols]
            b_last = b[sub - 1:sub, :]
            st = st_scr[h]
            o = _dot_nt((qs * jnp.exp(b)).astype(BF16), st.astype(BF16))
            for t in range(sub):
                bt = b[t:t + 1, :]
                kt = kk[t:t + 1, :]
                vt = vv[t:t + 1, :]
                w = jnp.exp(jnp.minimum(b - bt, 0.0))
                p = jnp.sum(qs * kt * w, axis=-1, keepdims=True)
                p = jnp.where(tidx >= t, p, 0.0)
                o = o + p * vt
            o_scr[pl.ds(r0, sub), cols] = o
            ke = kk * jnp.exp(b_last - b)
            st_scr[h] = st * jnp.exp(b_last) + _dot_tn(vv.astype(BF16), ke.astype(BF16))
        return carry

    lax.fori_loop(0, chunk // sub, sub_step, 0)

    gate = g_ref[...]
    gate = gate * jax.nn.sigmoid(gate)
    ng = ng_ref[...]
    for h in range(N_HEADS_B):
        cols = slice(h * DV_B, (h + 1) * DV_B)
        o = o_scr[:, cols]
        o = o * lax.rsqrt(jnp.mean(o * o, axis=-1, keepdims=True) + RMS_EPS) * ng
        o_ref[:, cols] = o * gate[:, cols]

    @pl.when(c == pl.num_programs(1) - 1)
    def _():
        s_ref[...] = st_scr[...]


def _hgrn(proj3, lb_row, norm_g_row, s0_t, batch, seq):
    chunk = min(128, seq)
    sub = min(SUBCHUNK_B, chunk)
    nc = seq // chunk
    n = batch * seq
    blk = lambda cb: pl.BlockSpec((None, chunk, COL), lambda b, c: (cb, b * nc + c, 0))
    st_spec = pl.BlockSpec((None, N_HEADS_B, DV_B, DK_B), lambda b, c: (b, 0, 0, 0))
    return pl.pallas_call(
        functools.partial(_hgrn_kernel, chunk=chunk, sub=sub),
        grid=(batch, nc),
        in_specs=[
            pl.BlockSpec((1, WIDTH_B), lambda b, c: (0, 0)),
            pl.BlockSpec((1, DV_B), lambda b, c: (0, 0)),
            blk(CB_QB), blk(CB_FB), blk(CB_IB), blk(CB_GB),
            st_spec,
        ],
        out_specs=[
            pl.BlockSpec((chunk, COL), lambda b, c: (b * nc + c, 0)),
            st_spec,
        ],
        out_shape=[
            jax.ShapeDtypeStruct((n, COL), F32),
            jax.ShapeDtypeStruct((batch, N_HEADS_B, DV_B, DK_B), F32),
        ],
        scratch_shapes=[
            pltpu.VMEM((N_HEADS_B, DV_B, DK_B), F32),
            pltpu.VMEM((chunk, WIDTH_B), F32),
            pltpu.VMEM((chunk, WIDTH_B), F32),
            pltpu.VMEM((chunk, WIDTH_B), F32),
            pltpu.VMEM((chunk, COL), F32),
        ],
        compiler_params=_cparams(("parallel", "arbitrary")),
        name="hgrn2",
    )(lb_row, norm_g_row, proj3, proj3, proj3, proj3, s0_t)


def _merge_ln_kernel(x_ref, oa_ref, ob_ref, za0_ref, za1_ref, zb0_ref, zb1_ref, bm_ref,
                     woa_ref, wob_ref, wout_ref, g_ref, b_ref, o_ref, *, alpha):
    ya = _dot(oa_ref[...].astype(BF16), woa_ref[...])
    yb = _dot(ob_ref[...].astype(BF16), wob_ref[...])
    za = jnp.concatenate([za0_ref[...], za1_ref[...]], axis=-1)
    zb = jnp.concatenate([zb0_ref[...], zb1_ref[...]], axis=-1)
    bm = bm_ref[...]
    m = jax.nn.sigmoid(za + bm[0:1, :]) * ya + jax.nn.sigmoid(zb + bm[1:2, :]) * yb
    mix = _dot(m.astype(BF16), wout_ref[...])
    o_ref[...] = _layer_norm(alpha * x_ref[...] + mix, g_ref[...], b_ref[...])


def _merge_ln(x, oa, ob, proj3, b_merge, w_oa, w_ob, w_out, ln_g, ln_b, layer, alpha):
    n = x.shape[0]
    tm = min(512, n)
    row = lambda w: pl.BlockSpec((tm, w), lambda i: (i, 0))
    zblk = lambda cb: pl.BlockSpec((None, tm, COL), lambda i: (cb, i, 0))
    wspec = lambda k: pl.BlockSpec((None, k, D_MODEL), lambda i: (layer, 0, 0))
    ln_spec = pl.BlockSpec((None, None, 1, D_MODEL), lambda i: (layer, 1, 0, 0))
    return pl.pallas_call(
        functools.partial(_merge_ln_kernel, alpha=alpha),
        grid=(n // tm,),
        in_specs=[
            row(D_MODEL), row(WIDTH_A), row(COL),
            zblk(CB_ZA), zblk(CB_ZA + 1), zblk(CB_ZB), zblk(CB_ZB + 1),
            pl.BlockSpec((None, 2, D_MODEL), lambda i: (layer, 0, 0)),
            wspec(WIDTH_A), wspec(COL), wspec(D_MODEL),
            ln_spec, ln_spec,
        ],
        out_specs=row(D_MODEL),
        out_shape=jax.ShapeDtypeStruct((n, D_MODEL), F32),
        compiler_params=_cparams(("parallel",)),
        name="merge_ln",
    )(x, oa, ob, proj3, proj3, proj3, proj3, b_merge, w_oa, w_ob, w_out, ln_g, ln_b)


def _embed_ln_kernel(x_ref, p_ref, wg_ref, wp_ref, bg_ref, g_ref, b_ref, o_ref, *, alpha):
    x = x_ref[...]
    gate = jax.nn.sigmoid(_dot(x.astype(BF16), wg_ref[...]) + bg_ref[...])
    e = gate * _dot(p_ref[...].astype(BF16), wp_ref[...])
    o_ref[...] = _layer_norm(alpha * x + e, g_ref[...], b_ref[...])


def _embed_ln(x, p, pe_w_gate, pe_w_proj, pe_b_gate, ln_g, ln_b, layer, alpha):
    n = x.shape[0]
    tm = min(512, n)
    ln_spec = pl.BlockSpec((None, None, 1, D_MODEL), lambda i: (layer, 3, 0, 0))
    return pl.pallas_call(
        functools.partial(_embed_ln_kernel, alpha=alpha),
        grid=(n // tm,),
        in_specs=[
            pl.BlockSpec((tm, D_MODEL), lambda i: (i, 0)),
            pl.BlockSpec((None, tm, P_DIM), lambda i: (layer, i, 0)),
            pl.BlockSpec((None, D_MODEL, D_MODEL), lambda i: (layer, 0, 0)),
            pl.BlockSpec((None, P_DIM, D_MODEL), lambda i: (layer, 0, 0)),
            pl.BlockSpec((None, 1, D_MODEL), lambda i: (layer, 0, 0)),
            ln_spec, ln_spec,
        ],
        out_specs=pl.BlockSpec((tm, D_MODEL), lambda i: (i, 0)),
        out_shape=jax.ShapeDtypeStruct((n, D_MODEL), F32),
        compiler_params=_cparams(("parallel",)),
        name="embed_ln",
    )(x, p, pe_w_gate, pe_w_proj, pe_b_gate, ln_g, ln_b)


def kernel(x_prompt, x_sample, cache_k, cache_v, state_hgrn, page_table, p_prompt, p_sample, ln_g, ln_b, ffn_w_gate, ffn_w_up, ffn_w_down, w_in, b_merge, sb_bias, hgrn_lb, hgrn_norm_g, w_oa, w_ob, w_out, pe_w_proj, pe_w_gate, pe_b_gate):
    depth = ln_g.shape[0]
    alpha = (2.0 * depth) ** 0.25
    batch, seq, _ = x_prompt.shape
    dec_b, dec_seq, _ = x_sample.shape
    n_pool, page = cache_k.shape[1], cache_k.shape[2]

    lb_all = jnp.cumsum(jax.nn.softmax(hgrn_lb.astype(F32), axis=0), axis=0)
    lb_all = lb_all - lb_all[0:1]

    wg = ffn_w_gate.astype(BF16)
    wu = ffn_w_up.astype(BF16)
    wd = ffn_w_down.astype(BF16)
    w_in_b = w_in.astype(BF16)
    w_oa_b = w_oa.astype(BF16)
    w_ob_b = w_ob.astype(BF16)
    w_out_b = w_out.astype(BF16)
    pe_wg_b = pe_w_gate.astype(BF16)
    pe_wp_b = pe_w_proj.astype(BF16)
    ln_g4 = ln_g.reshape(depth, 4, 1, D_MODEL)
    ln_b4 = ln_b.reshape(depth, 4, 1, D_MODEL)
    pe_bg3 = pe_b_gate.reshape(depth, 1, D_MODEL)
    cache_k4 = cache_k.reshape(depth, n_pool, page, WIDTH_A)
    cache_v4 = cache_v.reshape(depth, n_pool, page, WIDTH_A)
    p_prompt3 = p_prompt.reshape(depth, batch * seq, P_DIM)
    p_sample3 = p_sample.reshape(depth, dec_b * dec_seq, P_DIM)
    state_t = jnp.swapaxes(state_hgrn, -1, -2)
    zero_state = jnp.zeros((batch, N_HEADS_B, DV_B, DK_B), F32)

    def layer_rows(x, p3, i, b, l, s0_t, attn_fn):
        x = _ffn_ln(x, wg, wu, wd, ln_g4, ln_b4, i, 0, 0, alpha)
        proj3 = _in_proj(x, w_in_b, i)
        oa = attn_fn(proj3)
        ob, s_t = _hgrn(proj3, lb_all[i][None, :], hgrn_norm_g[i][None, :], s0_t, b, l)
        x = _merge_ln(x, oa, ob, proj3, b_merge, w_oa_b, w_ob_b, w_out_b, ln_g4, ln_b4, i, alpha)
        x = _ffn_ln(x, wg, wu, wd, ln_g4, ln_b4, i, 1, 2, alpha)
        x = _embed_ln(x, p3, pe_wg_b, pe_wp_b, pe_bg3, ln_g4, ln_b4, i, alpha)
        k_new = proj3[CB_KA].reshape(b, l, N_HEADS_A, HEAD_DIM_A)
        v_new = proj3[CB_VA].reshape(b, l, N_HEADS_A, HEAD_DIM_A)
        return x, k_new, v_new, jnp.swapaxes(s_t, -1, -2)

    yp = x_prompt.reshape(batch * seq, D_MODEL)
    ys = x_sample.reshape(dec_b * dec_seq, D_MODEL)
    kp, vp, sp, kd, vd, sd = [], [], [], [], [], []
    for i in range(depth):
        yp, k_new, v_new, s_new = layer_rows(
            yp, p_prompt3, i, batch, seq, zero_state,
            lambda proj3: _attn_prompt(proj3, sb_bias[i], batch, seq))
        kp.append(k_new)
        vp.append(v_new)
        sp.append(s_new)
        ys, k_new, v_new, s_new = layer_rows(
            ys, p_sample3, i, dec_b, dec_seq, state_t[i],
            lambda proj3: _attn_decode(proj3, cache_k4, cache_v4, page_table, sb_bias[i], i,
                                       dec_b, dec_seq))
        kd.append(k_new)
        vd.append(v_new)
        sd.append(s_new)
    return (yp.reshape(batch, seq, D_MODEL), ys.reshape(dec_b, dec_seq, D_MODEL),
            jnp.stack(kp), jnp.stack(vp), jnp.stack(sp),
            jnp.stack(kd), jnp.stack(vd), jnp.stack(sd))
```

```python
import functools
import math

import jax
import jax.numpy as jnp
from jax import lax
from jax.experimental import pallas as pl
from jax.experimental.pallas import tpu as pltpu

F32 = jnp.float32
BF16 = jnp.bfloat16

D_MODEL = 1024
N_HEADS_A = 8
HEAD_DIM_A = 64
WIDTH_A = N_HEADS_A * HEAD_DIM_A
N_HEADS_B = 4
DK_B = 128
DV_B = 128
WIDTH_B = N_HEADS_B * DK_B
D_FF = 2816
P_DIM = 256
LN_EPS = 1e-5
RMS_EPS = 1e-6
LOG2E = 1.4426950408889634
COL = 512
N_COLBLK = 11
N_SLOTS = N_COLBLK - 2
(SL_QA, SL_QB, SL_FB, SL_IB, SL_GB, SL_ZA, SL_ZB) = (0, 1, 2, 3, 4, 5, 7)

LANES = 128
MXU_DIM = 256
VMEM_LIMIT_BYTES = 56 * 1024 * 1024
FFN_TILE_F = D_FF // 2
SUBCHUNK_B = 16
PAGES_PER_STEP = 8
ATTN_Q_TILE = 512
ATTN_ROW_CHUNK = 256


def _cparams(sem):
    return pltpu.CompilerParams(dimension_semantics=sem, vmem_limit_bytes=VMEM_LIMIT_BYTES)


def _dot(a, b):
    return jnp.dot(a, b, preferred_element_type=F32)


def _dot_nt(a, b):
    return lax.dot_general(a, b, (((1,), (1,)), ((), ())), preferred_element_type=F32)


def _dot_tn(a, b):
    return lax.dot_general(a, b, (((0,), (0,)), ((), ())), preferred_element_type=F32)


def _split2(x):
    hi = x.astype(BF16)
    lo = (x - hi.astype(F32)).astype(BF16)
    return hi, lo


def _split3(x):
    h1 = x.astype(BF16)
    r = x - h1.astype(F32)
    h2 = r.astype(BF16)
    h3 = (r - h2.astype(F32)).astype(BF16)
    return h1, h2, h3


def _softplus(z):
    return jnp.maximum(z, 0.0) + jnp.log(1.0 + jnp.exp(-jnp.abs(z)))


def _softplus2(z2):
    return jnp.maximum(z2, 0.0) + jnp.log(1.0 + jnp.exp2(-jnp.abs(z2))) * LOG2E


def _layer_norm(y, g, b):
    mu = jnp.mean(y, axis=-1, keepdims=True)
    yc = y - mu
    var = jnp.mean(yc * yc, axis=-1, keepdims=True)
    return yc * lax.rsqrt(var + LN_EPS) * g + b


def _suffix_matrix(n):
    row = lax.broadcasted_iota(jnp.int32, (2 * n, n), 0)
    col = lax.broadcasted_iota(jnp.int32, (2 * n, n), 1)
    return jnp.where(jnp.where(row >= n, row - n, row) >= col, 1.0, 0.0).astype(BF16)


def _ffn_ln_kernel(x_ref, wg_ref, wu_ref, wd_ref, g_ref, b_ref, o_ref, acc_ref, *, alpha):
    j = pl.program_id(1)
    x = x_ref[...]
    xb = x.astype(BF16)
    g = _dot(xb, wg_ref[...])
    u = _dot(xb, wu_ref[...])
    h = (g * jax.nn.sigmoid(g) * u).astype(BF16)
    part = _dot(h, wd_ref[...])

    @pl.when(j == 0)
    def _():
        acc_ref[...] = part

    @pl.when(j > 0)
    def _():
        acc_ref[...] += part

    @pl.when(j == pl.num_programs(1) - 1)
    def _():
        y = alpha * x + 0.5 * acc_ref[...]
        o_ref[...] = _layer_norm(y, g_ref[...], b_ref[...])


def _ffn_ln(x, wg, wu, wd, ln_g, ln_b, layer, which, ln_idx, alpha):
    n = x.shape[0]
    tm = min(512, n)
    tf = FFN_TILE_F
    return pl.pallas_call(
        functools.partial(_ffn_ln_kernel, alpha=alpha),
        grid=(n // tm, D_FF // tf),
        in_specs=[
            pl.BlockSpec((tm, D_MODEL), lambda i, j: (i, 0)),
            pl.BlockSpec((None, None, D_MODEL, tf), lambda i, j: (layer, which, 0, j)),
            pl.BlockSpec((None, None, D_MODEL, tf), lambda i, j: (layer, which, 0, j)),
            pl.BlockSpec((None, None, tf, D_MODEL), lambda i, j: (layer, which, j, 0)),
            pl.BlockSpec((None, None, 1, D_MODEL), lambda i, j: (layer, ln_idx, 0, 0)),
            pl.BlockSpec((None, None, 1, D_MODEL), lambda i, j: (layer, ln_idx, 0, 0)),
        ],
        out_specs=pl.BlockSpec((tm, D_MODEL), lambda i, j: (i, 0)),
        out_shape=jax.ShapeDtypeStruct((n, D_MODEL), F32),
        scratch_shapes=[pltpu.VMEM((tm, D_MODEL), F32)],
        compiler_params=_cparams(("parallel", "arbitrary")),
        name="ffn_ln",
    )(x, wg, wu, wd, ln_g, ln_b)


def _slot_of_colblk(j):
    return jnp.where(j < 3, 0, j - 2)


def _in_proj_kernel(x_ref, w_ref, *rest, feature_major, n_alias):
    if feature_major:
        wkv_ref = rest[0]
        rest = rest[1:]
    o_ref, k_ref, v_ref, xb_scr = rest[n_alias:]
    j = pl.program_id(1)

    @pl.when(j == 0)
    def _():
        xb_scr[...] = x_ref[...].astype(BF16)

    def kv():
        if feature_major:
            return _dot_nt(wkv_ref[...], xb_scr[...])
        return _dot(xb_scr[...], w_ref[...])

    @pl.when(j == 1)
    def _():
        k_ref[...] = kv()

    @pl.when(j == 2)
    def _():
        v_ref[...] = kv()

    @pl.when((j == 0) | (j > 2))
    def _():
        o_ref[...] = _dot(xb_scr[...], w_ref[...])


def _in_proj_rows(x, w_in, layer):
    n = x.shape[0]
    tm = min(1024, n)
    kv_spec = pl.BlockSpec((tm, COL), lambda i, j: (i, 0))
    return pl.pallas_call(
        functools.partial(_in_proj_kernel, feature_major=False, n_alias=0),
        grid=(n // tm, N_COLBLK),
        in_specs=[
            pl.BlockSpec((tm, D_MODEL), lambda i, j: (i, 0)),
            pl.BlockSpec((None, D_MODEL, COL), lambda i, j: (layer, 0, j)),
        ],
        out_specs=[
            pl.BlockSpec((None, tm, COL), lambda i, j: (_slot_of_colblk(j), i, 0)),
            kv_spec, kv_spec,
        ],
        out_shape=[
            jax.ShapeDtypeStruct((N_SLOTS, n, COL), F32),
            jax.ShapeDtypeStruct((n, COL), F32),
            jax.ShapeDtypeStruct((n, COL), F32),
        ],
        scratch_shapes=[pltpu.VMEM((tm, D_MODEL), BF16)],
        compiler_params=_cparams(("parallel", "arbitrary")),
        name="in_proj_rows",
    )(x, w_in)


def _in_proj_prompt(x, w_in, w_kv_t, layer, depth, batch, seq, kt_prev, vt_prev):
    n = x.shape[0]
    tm = min(1024, seq)
    nt = seq // tm
    n_alias = 0 if kt_prev is None else 2
    kv_spec = pl.BlockSpec((None, None, COL, tm), lambda i, j: (layer, i // nt, 0, i % nt))
    in_specs = [
        pl.BlockSpec((tm, D_MODEL), lambda i, j: (i, 0)),
        pl.BlockSpec((None, D_MODEL, COL), lambda i, j: (layer, 0, j)),
        pl.BlockSpec((None, COL, D_MODEL), lambda i, j: (layer, jnp.clip(j - 1, 0, 1), 0)),
    ]
    args = [x, w_in, w_kv_t]
    aliases = {}
    if n_alias:
        in_specs += [pl.BlockSpec(memory_space=pl.ANY)] * 2
        args += [kt_prev, vt_prev]
        aliases = {3: 1, 4: 2}
    kv_shape = jax.ShapeDtypeStruct((depth, batch, COL, seq), F32)
    return pl.pallas_call(
        functools.partial(_in_proj_kernel, feature_major=True, n_alias=n_alias),
        grid=(n // tm, N_COLBLK),
        in_specs=in_specs,
        out_specs=[
            pl.BlockSpec((None, tm, COL), lambda i, j: (_slot_of_colblk(j), i, 0)),
            kv_spec, kv_spec,
        ],
        out_shape=[jax.ShapeDtypeStruct((N_SLOTS, n, COL), F32), kv_shape, kv_shape],
        scratch_shapes=[pltpu.VMEM((tm, D_MODEL), BF16)],
        input_output_aliases=aliases,
        compiler_params=_cparams(("parallel", "arbitrary")),
        name="in_proj_prompt",
    )(*args)


def _sb_weights(zs, suffix_m, cs, causals):
    blk = suffix_m.shape[1]
    nblk = zs[0].shape[1] // blk
    rows = zs[0].shape[0]
    nls = [_softplus2(z) for z in zs]
    nls = [nl if m is None else jnp.where(m, nl, 0.0) for nl, m in zip(nls, causals)]

    def stacked_split(nl):
        hi, lo = _split2(nl)
        return jnp.concatenate(
            [jnp.concatenate([hi[:, j * blk:(j + 1) * blk], lo[:, j * blk:(j + 1) * blk]], axis=1)
             for j in range(nblk)], axis=0)

    suffix = [_dot(stacked_split(nl), suffix_m) for nl in nls]
    out_a, out_c = [], []
    for z, nl, suf, c, m in zip(zs, nls, suffix, cs, causals):
        parts = [None] * nblk
        for j in reversed(range(nblk)):
            sl = slice(j * blk, (j + 1) * blk)
            parts[j] = jnp.exp2(z[:, sl] - suf[j * rows:(j + 1) * rows, :] - c)
            c = c + jnp.sum(nl[:, sl], axis=-1, keepdims=True)
        a = parts[0] if nblk == 1 else jnp.concatenate(parts, axis=1)
        out_a.append(a if m is None else jnp.where(m, a, 0.0))
        out_c.append(c)
    return out_a, out_c


def _attn_prompt_kernel(bias_ref, q_ref, kt_ref, vt_ref, o_ref, *, tile, row_chunk):
    pair = pl.program_id(1)
    qi = pl.program_id(2)
    q = q_ref[...] * (LOG2E / math.sqrt(HEAD_DIM_A))
    lane = lax.broadcasted_iota(jnp.int32, (tile, LANES), 1)
    q0 = jnp.where(lane < HEAD_DIM_A, q, 0.0)
    q_heads = (q0.astype(BF16), (q - q0).astype(BF16))
    biases = (bias_ref[2 * pair] * LOG2E, bias_ref[2 * pair + 1] * LOG2E)
    rc = min(row_chunk, tile)
    nrc = tile // rc
    chains = [(hh, r) for hh in range(2) for r in range(nrc)]
    q_chain = [q_heads[hh][r * rc:(r + 1) * rc] for hh, r in chains]
    row = lax.broadcasted_iota(jnp.int32, (rc, rc), 0)
    col = lax.broadcasted_iota(jnp.int32, (rc, rc), 1)
    causal = col < row
    suffix_m = _suffix_matrix(MXU_DIM if rc % MXU_DIM == 0 else rc)
    feat = lax.broadcasted_iota(jnp.int32, (LANES, rc), 0)
    head0_feat = feat < HEAD_DIM_A

    def step(kj, cs, os, diag):
        start = pl.multiple_of(kj * rc, rc)
        kt = kt_ref[:, pl.ds(start, rc)].astype(BF16)
        vt = vt_ref[:, pl.ds(start, rc)]
        vt_heads = (jnp.where(head0_feat, vt, 0.0).astype(BF16),
                    jnp.where(head0_feat, 0.0, vt).astype(BF16))
        live = [i for i, (_, r) in enumerate(chains) if diag is None or r >= diag]
        masks = [causal if chains[i][1] == diag else None for i in live]
        zs = [_dot(q_chain[i], kt) + biases[chains[i][0]] for i in live]
        a_s, c_new = _sb_weights(zs, suffix_m, [cs[i] for i in live], masks)
        cs, os = list(cs), list(os)
        for i, a, c in zip(live, a_s, c_new):
            hh, r = chains[i]
            cs[i] = c
            os[r] = os[r] + _dot_nt(a.astype(BF16), vt_heads[hh])
        return tuple(cs), tuple(os)

    cs = tuple(jnp.zeros((rc, 1), F32) for _ in chains)
    os = tuple(jnp.zeros((rc, LANES), F32) for _ in range(nrc))
    for d in reversed(range(nrc)):
        cs, os = step(qi * nrc + d, cs, os, d)
    cs, os = lax.fori_loop(0, qi * nrc,
                           lambda jj, co: step(qi * nrc - 1 - jj, co[0], co[1], None), (cs, os))
    for r in range(nrc):
        o_ref[r * rc:(r + 1) * rc, :] = os[r]


def _attn_prompt(slots, kt_all, vt_all, sb_bias, layer, batch, seq):
    tile = min(ATTN_Q_TILE, seq)
    nq = seq // tile
    n = batch * seq
    kv_spec = pl.BlockSpec((None, None, LANES, seq), lambda b, p, i: (layer, b, p, 0))
    return pl.pallas_call(
        functools.partial(_attn_prompt_kernel, tile=tile, row_chunk=ATTN_ROW_CHUNK),
        grid=(batch, WIDTH_A // LANES, nq),
        in_specs=[
            pl.BlockSpec(memory_space=pltpu.SMEM),
            pl.BlockSpec((None, tile, LANES), lambda b, p, i: (SL_QA, b * nq + i, p)),
            kv_spec, kv_spec,
        ],
        out_specs=pl.BlockSpec((tile, LANES), lambda b, p, i: (b * nq + i, p)),
        out_shape=jax.ShapeDtypeStruct((n, WIDTH_A), F32),
        compiler_params=_cparams(("parallel", "parallel", "arbitrary")),
        name="sb_attn_prompt",
    )(sb_bias, slots, kt_all, vt_all)


def _attn_decode_kernel(pt_ref, bias_ref, q_ref, kn_ref, vn_ref, *rest, n_new, page, pps):
    k_refs = rest[:pps]
    v_refs = rest[pps:2 * pps]
    o_ref = rest[2 * pps]
    qbd_scr, c_scr, acc_scr, kpad_scr, vpad_scr = rest[2 * pps + 1:]
    g = pl.program_id(1)
    rows = N_HEADS_A * n_new
    bias = bias_ref[...] * LOG2E

    @pl.when(g == 0)
    def _():
        q = q_ref[...] * (LOG2E / math.sqrt(HEAD_DIM_A))
        qt = jnp.concatenate([q] * N_HEADS_A, axis=0)
        r = lax.broadcasted_iota(jnp.int32, (rows, WIDTH_A), 0)
        f = lax.broadcasted_iota(jnp.int32, (rows, WIDTH_A), 1)
        qbd = jnp.where(r // n_new == f // HEAD_DIM_A, qt, 0.0).astype(BF16)
        qbd_scr[...] = qbd
        kpad_scr[...] = jnp.zeros_like(kpad_scr)
        vpad_scr[...] = jnp.zeros_like(vpad_scr)
        kpad_scr[0:n_new, :] = kn_ref[...]
        vpad_scr[0:n_new, :] = vn_ref[...]
        rr = lax.broadcasted_iota(jnp.int32, (rows, page), 0)
        ss = lax.broadcasted_iota(jnp.int32, (rows, page), 1)
        causal = ss < (rr % n_new)
        z = _dot_nt(qbd, kpad_scr[...].astype(BF16)) + bias
        (a,), (c,) = _sb_weights([z], _suffix_matrix(page), [jnp.zeros((rows, 1), F32)], [causal])
        acc_scr[...] = _dot(a.astype(BF16), vpad_scr[...].astype(BF16))
        c_scr[...] = c

    kt = jnp.concatenate([r[...] for r in k_refs], axis=1).astype(BF16)
    vt = jnp.concatenate([r[...] for r in v_refs], axis=1).astype(BF16)
    z = _dot(qbd_scr[...], kt) + bias
    blk = math.gcd(MXU_DIM, pps * page)
    (a,), (c,) = _sb_weights([z], _suffix_matrix(blk), [c_scr[...]], [None])
    acc_scr[...] += _dot_nt(a.astype(BF16), vt)
    c_scr[...] = c

    @pl.when(g == pl.num_programs(1) - 1)
    def _():
        acc = acc_scr[...]
        f = lax.broadcasted_iota(jnp.int32, (n_new, WIDTH_A), 1)
        out = jnp.zeros((n_new, WIDTH_A), F32)
        for h in range(N_HEADS_A):
            out = out + jnp.where(f // HEAD_DIM_A == h, acc[h * n_new:(h + 1) * n_new, :], 0.0)
        o_ref[...] = out


def _attn_decode(slots, k_new, v_new, cache_kt, cache_vt, page_table, sb_bias, layer, dec_b, n_new):
    page = cache_kt.shape[3]
    n_pages = page_table.shape[1]
    pps = math.gcd(PAGES_PER_STEP, n_pages)
    steps = n_pages // pps
    rows = N_HEADS_A * n_new
    bias_rows = jnp.repeat(sb_bias.astype(F32), n_new)[:, None]

    def page_spec(i):
        return pl.BlockSpec(
            (None, None, WIDTH_A, page),
            lambda b, g, pt: (layer, pt[b, n_pages - (g + 1) * pps + i], 0, 0))

    new_spec = pl.BlockSpec((n_new, COL), lambda b, g, pt: (b, 0))
    grid_spec = pltpu.PrefetchScalarGridSpec(
        num_scalar_prefetch=1,
        grid=(dec_b, steps),
        in_specs=[
            pl.BlockSpec((rows, 1), lambda b, g, pt: (0, 0)),
            pl.BlockSpec((None, n_new, COL), lambda b, g, pt: (SL_QA, b, 0)),
            new_spec, new_spec,
        ] + [page_spec(i) for i in range(pps)] + [page_spec(i) for i in range(pps)],
        out_specs=pl.BlockSpec((n_new, WIDTH_A), lambda b, g, pt: (b, 0)),
        scratch_shapes=[
            pltpu.VMEM((rows, WIDTH_A), BF16),
            pltpu.VMEM((rows, 1), F32),
            pltpu.VMEM((rows, WIDTH_A), F32),
            pltpu.VMEM((page, WIDTH_A), F32),
            pltpu.VMEM((page, WIDTH_A), F32),
        ],
    )
    return pl.pallas_call(
        functools.partial(_attn_decode_kernel, n_new=n_new, page=page, pps=pps),
        grid_spec=grid_spec,
        out_shape=jax.ShapeDtypeStruct((dec_b * n_new, WIDTH_A), F32),
        compiler_params=_cparams(("parallel", "arbitrary")),
        name="sb_attn_decode",
    )(page_table, bias_rows, slots, k_new, v_new, *([cache_kt] * pps), *([cache_vt] * pps))


def _hgrn_kernel(lb_ref, ng_ref, q_ref, f_ref, i_ref, g_ref, s0_ref, o_ref, s_ref,
                 st_scr, b_scr, q_scr, k_scr, o_scr, *, chunk, sub):
    c = pl.program_id(1)

    @pl.when(c == 0)
    def _():
        st_scr[...] = s0_ref[...]

    lb = lb_ref[...]
    zf = f_ref[...]
    a1 = jnp.log(lb)
    a2 = jnp.log1p(-lb) - _softplus(-zf)
    logf = jnp.maximum(a1, a2) + jnp.log(1.0 + jnp.exp(-jnp.abs(a1 - a2)))
    k_scr[...] = (1.0 - lb) * jax.nn.sigmoid(-zf)
    qv = q_ref[...]
    q_scr[...] = qv * jax.nn.sigmoid(qv)
    r = lax.broadcasted_iota(jnp.int32, (chunk, chunk), 0)
    s = lax.broadcasted_iota(jnp.int32, (chunk, chunk), 1)
    lmat = jnp.where((s <= r) & (s // sub == r // sub), 1.0, 0.0).astype(BF16)
    h1, h2, h3 = _split3(logf)
    b_scr[...] = _dot(lmat, h1) + _dot(lmat, h2) + _dot(lmat, h3)

    tidx = lax.broadcasted_iota(jnp.int32, (sub, 1), 0)

    def sub_step(ci, carry):
        r0 = pl.multiple_of(ci * sub, sub)
        for h in range(N_HEADS_B):
            cols = slice(h * DK_B, (h + 1) * DK_B)
            b = b_scr[pl.ds(r0, sub), cols]
            qs = q_scr[pl.ds(r0, sub), cols]
            kk = k_scr[pl.ds(r0, sub), cols]
            vv = i_ref[pl.ds(r0, sub), cols]
            b_last = b[sub - 1:sub, :]
            st = st_scr[h]
            o = _dot_nt((qs * jnp.exp(b)).astype(BF16), st.astype(BF16))
            for t in range(sub):
                bt = b[t:t + 1, :]
                kt = kk[t:t + 1, :]
                vt = vv[t:t + 1, :]
                w = jnp.exp(jnp.minimum(b - bt, 0.0))
                p = jnp.sum(qs * kt * w, axis=-1, keepdims=True)
                p = jnp.where(tidx >= t, p, 0.0)
                o = o + p * vt
            o_scr[pl.ds(r0, sub), cols] = o
            ke = kk * jnp.exp(b_last - b)
            st_scr[h] = st * jnp.exp(b_last) + _dot_tn(vv.astype(BF16), ke.astype(BF16))
        return carry

    lax.fori_loop(0, chunk // sub, sub_step, 0)

    gate = g_ref[...]
    gate = gate * jax.nn.sigmoid(gate)
    ng = ng_ref[...]
    for h in range(N_HEADS_B):
        cols = slice(h * DV_B, (h + 1) * DV_B)
        o = o_scr[:, cols]
        o = o * lax.rsqrt(jnp.mean(o * o, axis=-1, keepdims=True) + RMS_EPS) * ng
        o_ref[:, cols] = o * gate[:, cols]

    @pl.when(c == pl.num_programs(1) - 1)
    def _():
        s_ref[...] = st_scr[...]


def _hgrn(slots, lb_row, norm_g_row, s0_t, batch, seq):
    chunk = min(128, seq)
    sub = min(SUBCHUNK_B, chunk)
    nc = seq // chunk
    n = batch * seq
    blk = lambda sl: pl.BlockSpec((None, chunk, COL), lambda b, c: (sl, b * nc + c, 0))
    st_spec = pl.BlockSpec((None, N_HEADS_B, DV_B, DK_B), lambda b, c: (b, 0, 0, 0))
    return pl.pallas_call(
        functools.partial(_hgrn_kernel, chunk=chunk, sub=sub),
        grid=(batch, nc),
        in_specs=[
            pl.BlockSpec((1, WIDTH_B), lambda b, c: (0, 0)),
            pl.BlockSpec((1, DV_B), lambda b, c: (0, 0)),
            blk(SL_QB), blk(SL_FB), blk(SL_IB), blk(SL_GB),
            st_spec,
        ],
        out_specs=[
            pl.BlockSpec((chunk, COL), lambda b, c: (b * nc + c, 0)),
            st_spec,
        ],
        out_shape=[
            jax.ShapeDtypeStruct((n, COL), F32),
            jax.ShapeDtypeStruct((batch, N_HEADS_B, DV_B, DK_B), F32),
        ],
        scratch_shapes=[
            pltpu.VMEM((N_HEADS_B, DV_B, DK_B), F32),
            pltpu.VMEM((chunk, WIDTH_B), F32),
            pltpu.VMEM((chunk, WIDTH_B), F32),
            pltpu.VMEM((chunk, WIDTH_B), F32),
            pltpu.VMEM((chunk, COL), F32),
        ],
        compiler_params=_cparams(("parallel", "arbitrary")),
        name="hgrn2",
    )(lb_row, norm_g_row, slots, slots, slots, slots, s0_t)


def _merge_ln_kernel(x_ref, oa_ref, ob_ref, za0_ref, za1_ref, zb0_ref, zb1_ref, bm_ref,
                     woa_ref, wob_ref, wout_ref, g_ref, b_ref, o_ref, *, alpha):
    ya = _dot(oa_ref[...].astype(BF16), woa_ref[...])
    yb = _dot(ob_ref[...].astype(BF16), wob_ref[...])
    za = jnp.concatenate([za0_ref[...], za1_ref[...]], axis=-1)
    zb = jnp.concatenate([zb0_ref[...], zb1_ref[...]], axis=-1)
    bm = bm_ref[...]
    m = jax.nn.sigmoid(za + bm[0:1, :]) * ya + jax.nn.sigmoid(zb + bm[1:2, :]) * yb
    mix = _dot(m.astype(BF16), wout_ref[...])
    o_ref[...] = _layer_norm(alpha * x_ref[...] + mix, g_ref[...], b_ref[...])


def _merge_ln(x, oa, ob, slots, b_merge, w_oa, w_ob, w_out, ln_g, ln_b, layer, alpha):
    n = x.shape[0]
    tm = min(512, n)
    row = lambda w: pl.BlockSpec((tm, w), lambda i: (i, 0))
    zblk = lambda sl: pl.BlockSpec((None, tm, COL), lambda i: (sl, i, 0))
    wspec = lambda k: pl.BlockSpec((None, k, D_MODEL), lambda i: (layer, 0, 0))
    ln_spec = pl.BlockSpec((None, None, 1, D_MODEL), lambda i: (layer, 1, 0, 0))
    return pl.pallas_call(
        functools.partial(_merge_ln_kernel, alpha=alpha),
        grid=(n // tm,),
        in_specs=[
            row(D_MODEL), row(WIDTH_A), row(COL),
            zblk(SL_ZA), zblk(SL_ZA + 1), zblk(SL_ZB), zblk(SL_ZB + 1),
            pl.BlockSpec((None, 2, D_MODEL), lambda i: (layer, 0, 0)),
            wspec(WIDTH_A), wspec(COL), wspec(D_MODEL),
            ln_spec, ln_spec,
        ],
        out_specs=row(D_MODEL),
        out_shape=jax.ShapeDtypeStruct((n, D_MODEL), F32),
        compiler_params=_cparams(("parallel",)),
        name="merge_ln",
    )(x, oa, ob, slots, slots, slots, slots, b_merge, w_oa, w_ob, w_out, ln_g, ln_b)


def _embed_ln_kernel(x_ref, p_ref, wg_ref, wp_ref, bg_ref, g_ref, b_ref, o_ref, *, alpha):
    x = x_ref[...]
    gate = jax.nn.sigmoid(_dot(x.astype(BF16), wg_ref[...]) + bg_ref[...])
    e = gate * _dot(p_ref[...].astype(BF16), wp_ref[...])
    o_ref[...] = _layer_norm(alpha * x + e, g_ref[...], b_ref[...])


def _embed_ln(x, p, pe_w_gate, pe_w_proj, pe_b_gate, ln_g, ln_b, layer, alpha):
    n = x.shape[0]
    tm = min(512, n)
    ln_spec = pl.BlockSpec((None, None, 1, D_MODEL), lambda i: (layer, 3, 0, 0))
    return pl.pallas_call(
        functools.partial(_embed_ln_kernel, alpha=alpha),
        grid=(n // tm,),
        in_specs=[
            pl.BlockSpec((tm, D_MODEL), lambda i: (i, 0)),
            pl.BlockSpec((None, tm, P_DIM), lambda i: (layer, i, 0)),
            pl.BlockSpec((None, D_MODEL, D_MODEL), lambda i: (layer, 0, 0)),
            pl.BlockSpec((None, P_DIM, D_MODEL), lambda i: (layer, 0, 0)),
            pl.BlockSpec((None, 1, D_MODEL), lambda i: (layer, 0, 0)),
            ln_spec, ln_spec,
        ],
        out_specs=pl.BlockSpec((tm, D_MODEL), lambda i: (i, 0)),
        out_shape=jax.ShapeDtypeStruct((n, D_MODEL), F32),
        compiler_params=_cparams(("parallel",)),
        name="embed_ln",
    )(x, p, pe_w_gate, pe_w_proj, pe_b_gate, ln_g, ln_b)


def kernel(x_prompt, x_sample, cache_k, cache_v, state_hgrn, page_table, p_prompt, p_sample, ln_g, ln_b, ffn_w_gate, ffn_w_up, ffn_w_down, w_in, b_merge, sb_bias, hgrn_lb, hgrn_norm_g, w_oa, w_ob, w_out, pe_w_proj, pe_w_gate, pe_b_gate):
    depth = ln_g.shape[0]
    alpha = (2.0 * depth) ** 0.25
    batch, seq, _ = x_prompt.shape
    dec_b, dec_seq, _ = x_sample.shape
    n_pool, page = cache_k.shape[1], cache_k.shape[2]

    lb_all = jnp.cumsum(jax.nn.softmax(hgrn_lb.astype(F32), axis=0), axis=0)
    lb_all = lb_all - lb_all[0:1]

    wg = ffn_w_gate.astype(BF16)
    wu = ffn_w_up.astype(BF16)
    wd = ffn_w_down.astype(BF16)
    w_in_b = w_in.astype(BF16)
    w_kv_t = jnp.swapaxes(w_in_b[:, :, COL:3 * COL], 1, 2)
    w_oa_b = w_oa.astype(BF16)
    w_ob_b = w_ob.astype(BF16)
    w_out_b = w_out.astype(BF16)
    pe_wg_b = pe_w_gate.astype(BF16)
    pe_wp_b = pe_w_proj.astype(BF16)
    ln_g4 = ln_g.reshape(depth, 4, 1, D_MODEL)
    ln_b4 = ln_b.reshape(depth, 4, 1, D_MODEL)
    pe_bg3 = pe_b_gate.reshape(depth, 1, D_MODEL)
    cache_kt = jnp.transpose(cache_k, (0, 1, 3, 4, 2)).reshape(depth, n_pool, WIDTH_A, page)
    cache_vt = jnp.transpose(cache_v, (0, 1, 3, 4, 2)).reshape(depth, n_pool, WIDTH_A, page)
    p_prompt3 = p_prompt.reshape(depth, batch * seq, P_DIM)
    p_sample3 = p_sample.reshape(depth, dec_b * dec_seq, P_DIM)
    state_t = jnp.swapaxes(state_hgrn, -1, -2)
    zero_state = jnp.zeros((batch, N_HEADS_B, DV_B, DK_B), F32)

    def layer_tail(x, p3, i, slots, oa, ob):
        x = _merge_ln(x, oa, ob, slots, b_merge, w_oa_b, w_ob_b, w_out_b, ln_g4, ln_b4, i, alpha)
        x = _ffn_ln(x, wg, wu, wd, ln_g4, ln_b4, i, 1, 2, alpha)
        return _embed_ln(x, p3, pe_wg_b, pe_wp_b, pe_bg3, ln_g4, ln_b4, i, alpha)

    yp = x_prompt.reshape(batch * seq, D_MODEL)
    ys = x_sample.reshape(dec_b * dec_seq, D_MODEL)
    kt_all = vt_all = None
    sp, kd, vd, sd = [], [], [], []
    for i in range(depth):
        lb_row = lb_all[i][None, :]
        ng_row = hgrn_norm_g[i][None, :]
        yp = _ffn_ln(yp, wg, wu, wd, ln_g4, ln_b4, i, 0, 0, alpha)
        slots, kt_all, vt_all = _in_proj_prompt(yp, w_in_b, w_kv_t, i, depth, batch, seq, kt_all, vt_all)
        oa = _attn_prompt(slots, kt_all, vt_all, sb_bias[i], i, batch, seq)
        ob, s_t = _hgrn(slots, lb_row, ng_row, zero_state, batch, seq)
        yp = layer_tail(yp, p_prompt3, i, slots, oa, ob)
        sp.append(jnp.swapaxes(s_t, -1, -2))
        ys = _ffn_ln(ys, wg, wu, wd, ln_g4, ln_b4, i, 0, 0, alpha)
        slots, k_new, v_new = _in_proj_rows(ys, w_in_b, i)
        oa = _attn_decode(slots, k_new, v_new, cache_kt, cache_vt, page_table, sb_bias[i], i,
                          dec_b, dec_seq)
        ob, s_t = _hgrn(slots, lb_row, ng_row, state_t[i], dec_b, dec_seq)
        ys = layer_tail(ys, p_sample3, i, slots, oa, ob)
        kd.append(k_new.reshape(dec_b, dec_seq, N_HEADS_A, HEAD_DIM_A))
        vd.append(v_new.reshape(dec_b, dec_seq, N_HEADS_A, HEAD_DIM_A))
        sd.append(jnp.swapaxes(s_t, -1, -2))

    def token_major(t_all):
        t5 = t_all.reshape(depth, batch, N_HEADS_A, HEAD_DIM_A, seq)
        return jnp.transpose(t5, (0, 1, 4, 2, 3))

    return (yp.reshape(batch, seq, D_MODEL), ys.reshape(dec_b, dec_seq, D_MODEL),
            token_major(kt_all), token_major(vt_all), jnp.stack(sp),
            jnp.stack(kd), jnp.stack(vd), jnp.stack(sd))
```

```python
import functools
import math

import jax
import jax.numpy as jnp
from jax import lax
from jax.experimental import pallas as pl
from jax.experimental.pallas import tpu as pltpu

F32 = jnp.float32
BF16 = jnp.bfloat16

D_MODEL = 1024
N_HEADS_A = 8
HEAD_DIM_A = 64
WIDTH_A = N_HEADS_A * HEAD_DIM_A
N_HEADS_B = 4
DK_B = 128
DV_B = 128
WIDTH_B = N_HEADS_B * DK_B
D_FF = 2816
P_DIM = 256
LN_EPS = 1e-5
RMS_EPS = 1e-6
LOG2E = 1.4426950408889634
COL = 512
N_COLBLK = 11
N_SLOTS = N_COLBLK - 2
(SL_QA, SL_QB, SL_FB, SL_IB, SL_GB, SL_ZA, SL_ZB) = (0, 1, 2, 3, 4, 5, 7)

LANES = 128
MXU_DIM = 256
VMEM_LIMIT_BYTES = 56 * 1024 * 1024
FFN_TILE_F = D_FF // 2
SUBCHUNK_B = 16
PAGES_PER_STEP = 16
ATTN_Q_TILE = 1024
ATTN_ROW_CHUNK = 256


def _cparams(sem):
    return pltpu.CompilerParams(dimension_semantics=sem, vmem_limit_bytes=VMEM_LIMIT_BYTES)


def _dot(a, b):
    return jnp.dot(a, b, preferred_element_type=F32)


def _dot_nt(a, b):
    return lax.dot_general(a, b, (((1,), (1,)), ((), ())), preferred_element_type=F32)


def _dot_tn(a, b):
    return lax.dot_general(a, b, (((0,), (0,)), ((), ())), preferred_element_type=F32)


def _split2(x):
    hi = x.astype(BF16)
    lo = (x - hi.astype(F32)).astype(BF16)
    return hi, lo


def _split3(x):
    h1 = x.astype(BF16)
    r = x - h1.astype(F32)
    h2 = r.astype(BF16)
    h3 = (r - h2.astype(F32)).astype(BF16)
    return h1, h2, h3


def _softplus(z):
    return jnp.maximum(z, 0.0) + jnp.log(1.0 + jnp.exp(-jnp.abs(z)))


def _softplus2(z2):
    return jnp.maximum(z2, 0.0) + jnp.log(1.0 + jnp.exp2(-jnp.abs(z2))) * LOG2E


def _layer_norm(y, g, b):
    mu = jnp.mean(y, axis=-1, keepdims=True)
    yc = y - mu
    var = jnp.mean(yc * yc, axis=-1, keepdims=True)
    return yc * lax.rsqrt(var + LN_EPS) * g + b


def _suffix_matrix(n, halves):
    row = lax.broadcasted_iota(jnp.int32, (halves * n, n), 0)
    col = lax.broadcasted_iota(jnp.int32, (halves * n, n), 1)
    return jnp.where(jnp.where(row >= n, row - n, row) >= col, 1.0, 0.0).astype(BF16)


def _ffn_ln_kernel(x_ref, wg_ref, wu_ref, wd_ref, g_ref, b_ref, o_ref, acc_ref, *, alpha):
    j = pl.program_id(1)
    x = x_ref[...]
    xb = x.astype(BF16)
    g = _dot(xb, wg_ref[...])
    u = _dot(xb, wu_ref[...])
    h = (g * jax.nn.sigmoid(g) * u).astype(BF16)
    part = _dot(h, wd_ref[...])

    @pl.when(j == 0)
    def _():
        acc_ref[...] = part

    @pl.when(j > 0)
    def _():
        acc_ref[...] += part

    @pl.when(j == pl.num_programs(1) - 1)
    def _():
        y = alpha * x + 0.5 * acc_ref[...]
        o_ref[...] = _layer_norm(y, g_ref[...], b_ref[...])


def _ffn_ln(x, wg, wu, wd, ln_g, ln_b, layer, which, ln_idx, alpha):
    n = x.shape[0]
    tm = min(512, n)
    tf = FFN_TILE_F
    return pl.pallas_call(
        functools.partial(_ffn_ln_kernel, alpha=alpha),
        grid=(n // tm, D_FF // tf),
        in_specs=[
            pl.BlockSpec((tm, D_MODEL), lambda i, j: (i, 0)),
            pl.BlockSpec((None, None, D_MODEL, tf), lambda i, j: (layer, which, 0, j)),
            pl.BlockSpec((None, None, D_MODEL, tf), lambda i, j: (layer, which, 0, j)),
            pl.BlockSpec((None, None, tf, D_MODEL), lambda i, j: (layer, which, j, 0)),
            pl.BlockSpec((None, None, 1, D_MODEL), lambda i, j: (layer, ln_idx, 0, 0)),
            pl.BlockSpec((None, None, 1, D_MODEL), lambda i, j: (layer, ln_idx, 0, 0)),
        ],
        out_specs=pl.BlockSpec((tm, D_MODEL), lambda i, j: (i, 0)),
        out_shape=jax.ShapeDtypeStruct((n, D_MODEL), F32),
        scratch_shapes=[pltpu.VMEM((tm, D_MODEL), F32)],
        compiler_params=_cparams(("parallel", "arbitrary")),
        name="ffn_ln",
    )(x, wg, wu, wd, ln_g, ln_b)


def _slot_of_colblk(j):
    return jnp.where(j < 3, 0, j - 2)


def _in_proj_kernel(x_ref, w_ref, *rest, feature_major, n_alias):
    if feature_major:
        wkv_ref = rest[0]
        rest = rest[1:]
    o_ref, k_ref, v_ref, xb_scr = rest[n_alias:]
    j = pl.program_id(1)

    @pl.when(j == 0)
    def _():
        xb_scr[...] = x_ref[...].astype(BF16)

    def kv():
        if feature_major:
            return _dot_nt(wkv_ref[...], xb_scr[...])
        return _dot(xb_scr[...], w_ref[...])

    @pl.when(j == 1)
    def _():
        k_ref[...] = kv()

    @pl.when(j == 2)
    def _():
        v_ref[...] = kv()

    @pl.when((j == 0) | (j > 2))
    def _():
        o_ref[...] = _dot(xb_scr[...], w_ref[...])


def _in_proj_rows(x, w_in, layer):
    n = x.shape[0]
    tm = min(1024, n)
    kv_spec = pl.BlockSpec((tm, COL), lambda i, j: (i, 0))
    return pl.pallas_call(
        functools.partial(_in_proj_kernel, feature_major=False, n_alias=0),
        grid=(n // tm, N_COLBLK),
        in_specs=[
            pl.BlockSpec((tm, D_MODEL), lambda i, j: (i, 0)),
            pl.BlockSpec((None, D_MODEL, COL), lambda i, j: (layer, 0, j)),
        ],
        out_specs=[
            pl.BlockSpec((None, tm, COL), lambda i, j: (_slot_of_colblk(j), i, 0)),
            kv_spec, kv_spec,
        ],
        out_shape=[
            jax.ShapeDtypeStruct((N_SLOTS, n, COL), F32),
            jax.ShapeDtypeStruct((n, COL), F32),
            jax.ShapeDtypeStruct((n, COL), F32),
        ],
        scratch_shapes=[pltpu.VMEM((tm, D_MODEL), BF16)],
        compiler_params=_cparams(("parallel", "arbitrary")),
        name="in_proj_rows",
    )(x, w_in)


def _in_proj_prompt(x, w_in, w_kv_t, layer, depth, batch, seq, kt_prev, vt_prev):
    n = x.shape[0]
    tm = min(1024, seq)
    nt = seq // tm
    n_alias = 0 if kt_prev is None else 2
    kv_spec = pl.BlockSpec((None, None, COL, tm), lambda i, j: (layer, i // nt, 0, i % nt))
    in_specs = [
        pl.BlockSpec((tm, D_MODEL), lambda i, j: (i, 0)),
        pl.BlockSpec((None, D_MODEL, COL), lambda i, j: (layer, 0, j)),
        pl.BlockSpec((None, COL, D_MODEL), lambda i, j: (layer, jnp.clip(j - 1, 0, 1), 0)),
    ]
    args = [x, w_in, w_kv_t]
    aliases = {}
    if n_alias:
        in_specs += [pl.BlockSpec(memory_space=pl.ANY)] * 2
        args += [kt_prev, vt_prev]
        aliases = {3: 1, 4: 2}
    kv_shape = jax.ShapeDtypeStruct((depth, batch, COL, seq), F32)
    return pl.pallas_call(
        functools.partial(_in_proj_kernel, feature_major=True, n_alias=n_alias),
        grid=(n // tm, N_COLBLK),
        in_specs=in_specs,
        out_specs=[
            pl.BlockSpec((None, tm, COL), lambda i, j: (_slot_of_colblk(j), i, 0)),
            kv_spec, kv_spec,
        ],
        out_shape=[jax.ShapeDtypeStruct((N_SLOTS, n, COL), F32), kv_shape, kv_shape],
        scratch_shapes=[pltpu.VMEM((tm, D_MODEL), BF16)],
        input_output_aliases=aliases,
        compiler_params=_cparams(("parallel", "arbitrary")),
        name="in_proj_prompt",
    )(*args)


def _sb_suffix(zs, suffix_m, causals):
    blk = suffix_m.shape[1]
    nblk = zs[0].shape[1] // blk
    nls = [_softplus2(z) for z in zs]
    nls = [nl if m is None else jnp.where(m, nl, 0.0) for nl, m in zip(nls, causals)]

    def stacked_split(nl):
        halves = _split2(nl) if suffix_m.shape[0] == 2 * blk else (nl.astype(BF16),)
        return jnp.concatenate(
            [jnp.concatenate([h[:, j * blk:(j + 1) * blk] for h in halves], axis=1)
             for j in range(nblk)], axis=0)

    return nls, [_dot(stacked_split(nl), suffix_m) for nl in nls]


def _sb_finish(zs, nls, suffix, cs, causals, chained=False):
    blk = suffix[0].shape[1]
    nblk = zs[0].shape[1] // blk
    rows = zs[0].shape[0]
    out_a, out_c = [], []
    c = None
    for z, nl, suf, c_in, m in zip(zs, nls, suffix, cs, causals):
        c = c_in if (c is None or not chained) else c
        parts = [None] * nblk
        for j in reversed(range(nblk)):
            sl = slice(j * blk, (j + 1) * blk)
            parts[j] = jnp.exp2(z[:, sl] - suf[j * rows:(j + 1) * rows, :] - c)
            c = c + jnp.sum(nl[:, sl], axis=-1, keepdims=True)
        a = parts[0] if nblk == 1 else jnp.concatenate(parts, axis=1)
        out_a.append(a if m is None else jnp.where(m, a, 0.0))
        out_c.append(c)
    return out_a, out_c


def _attn_prompt_kernel(bias_ref, q_ref, kt_ref, vt_ref, o_ref, *, tile, row_chunk):
    pair = pl.program_id(1)
    qi = pl.program_id(2)
    q = q_ref[...] * (LOG2E / math.sqrt(HEAD_DIM_A))
    lane = lax.broadcasted_iota(jnp.int32, (tile, LANES), 1)
    q0 = jnp.where(lane < HEAD_DIM_A, q, 0.0)
    q_heads = (q0.astype(BF16), (q - q0).astype(BF16))
    biases = (bias_ref[2 * pair] * LOG2E, bias_ref[2 * pair + 1] * LOG2E)
    rc = min(row_chunk, tile)
    nrc = tile // rc
    chains = [(hh, r) for hh in range(2) for r in range(nrc)]
    q_chain = [q_heads[hh][r * rc:(r + 1) * rc] for hh, r in chains]
    row = lax.broadcasted_iota(jnp.int32, (rc, rc), 0)
    col = lax.broadcasted_iota(jnp.int32, (rc, rc), 1)
    causal = col < row
    suffix_m = _suffix_matrix(MXU_DIM if rc % MXU_DIM == 0 else rc, 1)
    feat = lax.broadcasted_iota(jnp.int32, (LANES, rc), 0)
    head0_feat = feat < HEAD_DIM_A

    def step(kj, cs, ots, diag):
        start = pl.multiple_of(kj * rc, rc)
        kt = kt_ref[:, pl.ds(start, rc)].astype(BF16)
        vt = vt_ref[:, pl.ds(start, rc)]
        vt_heads = (jnp.where(head0_feat, vt, 0.0).astype(BF16),
                    jnp.where(head0_feat, 0.0, vt).astype(BF16))
        live = [i for i, (_, r) in enumerate(chains) if diag is None or r >= diag]
        masks = [causal if chains[i][1] == diag else None for i in live]
        zs = [_dot(q_chain[i], kt) + biases[chains[i][0]] for i in live]
        nls, suffix = _sb_suffix(zs, suffix_m, masks)
        a_s, c_new = _sb_finish(zs, nls, suffix, [cs[i] for i in live], masks)
        cs, ots = list(cs), list(ots)
        for i, a, c in zip(live, a_s, c_new):
            hh, r = chains[i]
            cs[i] = c
            ots[r] = ots[r] + _dot_nt(vt_heads[hh], a.astype(BF16))
        return tuple(cs), tuple(ots)

    cs = tuple(jnp.zeros((rc, 1), F32) for _ in chains)
    ots = tuple(jnp.zeros((LANES, rc), F32) for _ in range(nrc))
    for d in reversed(range(nrc)):
        cs, ots = step(qi * nrc + d, cs, ots, d)
    cs, ots = lax.fori_loop(0, qi * nrc,
                            lambda jj, co: step(qi * nrc - 1 - jj, co[0], co[1], None), (cs, ots))
    for r in range(nrc):
        o_ref[r * rc:(r + 1) * rc, :] = ots[r].T


def _attn_prompt(slots, kt_all, vt_all, sb_bias, layer, batch, seq):
    tile = min(ATTN_Q_TILE, seq)
    nq = seq // tile
    n = batch * seq
    kv_spec = pl.BlockSpec((None, None, LANES, seq), lambda b, p, i: (layer, b, p, 0))
    return pl.pallas_call(
        functools.partial(_attn_prompt_kernel, tile=tile, row_chunk=ATTN_ROW_CHUNK),
        grid=(batch, WIDTH_A // LANES, nq),
        in_specs=[
            pl.BlockSpec(memory_space=pltpu.SMEM),
            pl.BlockSpec((None, tile, LANES), lambda b, p, i: (SL_QA, b * nq + i, p)),
            kv_spec, kv_spec,
        ],
        out_specs=pl.BlockSpec((tile, LANES), lambda b, p, i: (b * nq + i, p)),
        out_shape=jax.ShapeDtypeStruct((n, WIDTH_A), F32),
        compiler_params=_cparams(("parallel", "parallel", "arbitrary")),
        name="sb_attn_prompt",
    )(sb_bias, slots, kt_all, vt_all)


def _attn_decode_kernel(pt_ref, bias_ref, q_ref, kn_ref, vn_ref, *rest, n_new, page, pps):
    k_refs = rest[:pps]
    v_refs = rest[pps:2 * pps]
    o_ref = rest[2 * pps]
    qbd_scr, c_scr, acc_scr, kpad_scr, vpad_scr = rest[2 * pps + 1:]
    g = pl.program_id(1)
    rows = N_HEADS_A * n_new
    bias = bias_ref[...] * LOG2E

    @pl.when(g == 0)
    def _():
        q = q_ref[...] * (LOG2E / math.sqrt(HEAD_DIM_A))
        qt = jnp.concatenate([q] * N_HEADS_A, axis=0)
        r = lax.broadcasted_iota(jnp.int32, (rows, WIDTH_A), 0)
        f = lax.broadcasted_iota(jnp.int32, (rows, WIDTH_A), 1)
        qbd = jnp.where(r // n_new == f // HEAD_DIM_A, qt, 0.0).astype(BF16)
        qbd_scr[...] = qbd
        kpad_scr[...] = jnp.zeros_like(kpad_scr)
        vpad_scr[...] = jnp.zeros_like(vpad_scr)
        kpad_scr[0:n_new, :] = kn_ref[...]
        vpad_scr[0:n_new, :] = vn_ref[...]
        rr = lax.broadcasted_iota(jnp.int32, (rows, page), 0)
        ss = lax.broadcasted_iota(jnp.int32, (rows, page), 1)
        causal = ss < (rr % n_new)
        z = _dot_nt(qbd, kpad_scr[...].astype(BF16)) + bias
        nls, suffix = _sb_suffix([z], _suffix_matrix(page, 2), [causal])
        (a,), (c,) = _sb_finish([z], nls, suffix, [jnp.zeros((rows, 1), F32)], [causal])
        acc_scr[...] = _dot(a.astype(BF16), vpad_scr[...].astype(BF16))
        c_scr[...] = c

    kt = jnp.concatenate([r[...] for r in k_refs], axis=1).astype(BF16)
    vt = jnp.concatenate([r[...] for r in v_refs], axis=1).astype(BF16)
    z = _dot(qbd_scr[...], kt) + bias
    blk = math.gcd(MXU_DIM, pps * page)
    nls, suffix = _sb_suffix([z], _suffix_matrix(blk, 2), [None])
    (a,), (c,) = _sb_finish([z], nls, suffix, [c_scr[...]], [None])
    acc_scr[...] += _dot_nt(a.astype(BF16), vt)
    c_scr[...] = c

    @pl.when(g == pl.num_programs(1) - 1)
    def _():
        acc = acc_scr[...]
        f = lax.broadcasted_iota(jnp.int32, (n_new, WIDTH_A), 1)
        out = jnp.zeros((n_new, WIDTH_A), F32)
        for h in range(N_HEADS_A):
            out = out + jnp.where(f // HEAD_DIM_A == h, acc[h * n_new:(h + 1) * n_new, :], 0.0)
        o_ref[...] = out


def _attn_decode(slots, k_new, v_new, cache_kt, cache_vt, page_table, sb_bias, layer, dec_b, n_new):
    page = cache_kt.shape[3]
    n_pages = page_table.shape[1]
    pps = math.gcd(PAGES_PER_STEP, n_pages)
    steps = n_pages // pps
    rows = N_HEADS_A * n_new
    bias_rows = jnp.repeat(sb_bias.astype(F32), n_new)[:, None]

    def page_spec(i):
        return pl.BlockSpec(
            (None, None, WIDTH_A, page),
            lambda b, g, pt: (layer, pt[b, n_pages - (g + 1) * pps + i], 0, 0))

    new_spec = pl.BlockSpec((n_new, COL), lambda b, g, pt: (b, 0))
    grid_spec = pltpu.PrefetchScalarGridSpec(
        num_scalar_prefetch=1,
        grid=(dec_b, steps),
        in_specs=[
            pl.BlockSpec((rows, 1), lambda b, g, pt: (0, 0)),
            pl.BlockSpec((None, n_new, COL), lambda b, g, pt: (SL_QA, b, 0)),
            new_spec, new_spec,
        ] + [page_spec(i) for i in range(pps)] + [page_spec(i) for i in range(pps)],
        out_specs=pl.BlockSpec((n_new, WIDTH_A), lambda b, g, pt: (b, 0)),
        scratch_shapes=[
            pltpu.VMEM((rows, WIDTH_A), BF16),
            pltpu.VMEM((rows, 1), F32),
            pltpu.VMEM((rows, WIDTH_A), F32),
            pltpu.VMEM((page, WIDTH_A), F32),
            pltpu.VMEM((page, WIDTH_A), F32),
        ],
    )
    return pl.pallas_call(
        functools.partial(_attn_decode_kernel, n_new=n_new, page=page, pps=pps),
        grid_spec=grid_spec,
        out_shape=jax.ShapeDtypeStruct((dec_b * n_new, WIDTH_A), F32),
        compiler_params=_cparams(("parallel", "arbitrary")),
        name="sb_attn_decode",
    )(page_table, bias_rows, slots, k_new, v_new, *([cache_kt] * pps), *([cache_vt] * pps))


def _hgrn_kernel(lb_ref, ng_ref, q_ref, f_ref, i_ref, g_ref, s0_ref, o_ref, s_ref,
                 st_scr, b_scr, q_scr, k_scr, o_scr, *, chunk, sub):
    c = pl.program_id(1)

    @pl.when(c == 0)
    def _():
        st_scr[...] = s0_ref[...]

    lb = lb_ref[...]
    zf = f_ref[...]
    a1 = jnp.log(lb)
    a2 = jnp.log1p(-lb) - _softplus(-zf)
    logf = jnp.maximum(a1, a2) + jnp.log(1.0 + jnp.exp(-jnp.abs(a1 - a2)))
    k_scr[...] = (1.0 - lb) * jax.nn.sigmoid(-zf)
    qv = q_ref[...]
    q_scr[...] = qv * jax.nn.sigmoid(qv)
    r = lax.broadcasted_iota(jnp.int32, (chunk, chunk), 0)
    s = lax.broadcasted_iota(jnp.int32, (chunk, chunk), 1)
    lmat = jnp.where((s <= r) & (s // sub == r // sub), 1.0, 0.0).astype(BF16)
    h1, h2, h3 = _split3(logf)
    b_scr[...] = _dot(lmat, h1) + _dot(lmat, h2) + _dot(lmat, h3)

    tidx = lax.broadcasted_iota(jnp.int32, (sub, 1), 0)

    def sub_step(ci, carry):
        r0 = pl.multiple_of(ci * sub, sub)
        for h in range(N_HEADS_B):
            cols = slice(h * DK_B, (h + 1) * DK_B)
            b = b_scr[pl.ds(r0, sub), cols]
            qs = q_scr[pl.ds(r0, sub), cols]
            kk = k_scr[pl.ds(r0, sub), cols]
            vv = i_ref[pl.ds(r0, sub), cols]
            b_last = b[sub - 1:sub, :]
            st = st_scr[h]
            o = _dot_nt((qs * jnp.exp(b)).astype(BF16), st.astype(BF16))
            for t in range(sub):
                bt = b[t:t + 1, :]
                kt = kk[t:t + 1, :]
                vt = vv[t:t + 1, :]
                w = jnp.exp(jnp.minimum(b - bt, 0.0))
                p = jnp.sum(qs * kt * w, axis=-1, keepdims=True)
                p = jnp.where(tidx >= t, p, 0.0)
                o = o + p * vt
            o_scr[pl.ds(r0, sub), cols] = o
            ke = kk * jnp.exp(b_last - b)
            st_scr[h] = st * jnp.exp(b_last) + _dot_tn(vv.astype(BF16), ke.astype(BF16))
        return carry

    lax.fori_loop(0, chunk // sub, sub_step, 0)

    gate = g_ref[...]
    gate = gate * jax.nn.sigmoid(gate)
    ng = ng_ref[...]
    for h in range(N_HEADS_B):
        cols = slice(h * DV_B, (h + 1) * DV_B)
        o = o_scr[:, cols]
        o = o * lax.rsqrt(jnp.mean(o * o, axis=-1, keepdims=True) + RMS_EPS) * ng
        o_ref[:, cols] = o * gate[:, cols]

    @pl.when(c == pl.num_programs(1) - 1)
    def _():
        s_ref[...] = st_scr[...]


def _hgrn(slots, lb_row, norm_g_row, s0_t, batch, seq):
    chunk = min(128, seq)
    sub = min(SUBCHUNK_B, chunk)
    nc = seq // chunk
    n = batch * seq
    blk = lambda sl: pl.BlockSpec((None, chunk, COL), lambda b, c: (sl, b * nc + c, 0))
    st_spec = pl.BlockSpec((None, N_HEADS_B, DV_B, DK_B), lambda b, c: (b, 0, 0, 0))
    return pl.pallas_call(
        functools.partial(_hgrn_kernel, chunk=chunk, sub=sub),
        grid=(batch, nc),
        in_specs=[
            pl.BlockSpec((1, WIDTH_B), lambda b, c: (0, 0)),
            pl.BlockSpec((1, DV_B), lambda b, c: (0, 0)),
            blk(SL_QB), blk(SL_FB), blk(SL_IB), blk(SL_GB),
            st_spec,
        ],
        out_specs=[
            pl.BlockSpec((chunk, COL), lambda b, c: (b * nc + c, 0)),
            st_spec,
        ],
        out_shape=[
            jax.ShapeDtypeStruct((n, COL), F32),
            jax.ShapeDtypeStruct((batch, N_HEADS_B, DV_B, DK_B), F32),
        ],
        scratch_shapes=[
            pltpu.VMEM((N_HEADS_B, DV_B, DK_B), F32),
            pltpu.VMEM((chunk, WIDTH_B), F32),
            pltpu.VMEM((chunk, WIDTH_B), F32),
            pltpu.VMEM((chunk, WIDTH_B), F32),
            pltpu.VMEM((chunk, COL), F32),
        ],
        compiler_params=_cparams(("parallel", "arbitrary")),
        name="hgrn2",
    )(lb_row, norm_g_row, slots, slots, slots, slots, s0_t)


def _merge_ln_kernel(x_ref, oa_ref, ob_ref, za0_ref, za1_ref, zb0_ref, zb1_ref, bm_ref,
                     woa_ref, wob_ref, wout_ref, g_ref, b_ref, o_ref, *, alpha):
    ya = _dot(oa_ref[...].astype(BF16), woa_ref[...])
    yb = _dot(ob_ref[...].astype(BF16), wob_ref[...])
    za = jnp.concatenate([za0_ref[...], za1_ref[...]], axis=-1)
    zb = jnp.concatenate([zb0_ref[...], zb1_ref[...]], axis=-1)
    bm = bm_ref[...]
    m = jax.nn.sigmoid(za + bm[0:1, :]) * ya + jax.nn.sigmoid(zb + bm[1:2, :]) * yb
    mix = _dot(m.astype(BF16), wout_ref[...])
    o_ref[...] = _layer_norm(alpha * x_ref[...] + mix, g_ref[...], b_ref[...])


def _merge_ln(x, oa, ob, slots, b_merge, w_oa, w_ob, w_out, ln_g, ln_b, layer, alpha):
    n = x.shape[0]
    tm = min(512, n)
    row = lambda w: pl.BlockSpec((tm, w), lambda i: (i, 0))
    zblk = lambda sl: pl.BlockSpec((None, tm, COL), lambda i: (sl, i, 0))
    wspec = lambda k: pl.BlockSpec((None, k, D_MODEL), lambda i: (layer, 0, 0))
    ln_spec = pl.BlockSpec((None, None, 1, D_MODEL), lambda i: (layer, 1, 0, 0))
    return pl.pallas_call(
        functools.partial(_merge_ln_kernel, alpha=alpha),
        grid=(n // tm,),
        in_specs=[
            row(D_MODEL), row(WIDTH_A), row(COL),
            zblk(SL_ZA), zblk(SL_ZA + 1), zblk(SL_ZB), zblk(SL_ZB + 1),
            pl.BlockSpec((None, 2, D_MODEL), lambda i: (layer, 0, 0)),
            wspec(WIDTH_A), wspec(COL), wspec(D_MODEL),
            ln_spec, ln_spec,
        ],
        out_specs=row(D_MODEL),
        out_shape=jax.ShapeDtypeStruct((n, D_MODEL), F32),
        compiler_params=_cparams(("parallel",)),
        name="merge_ln",
    )(x, oa, ob, slots, slots, slots, slots, b_merge, w_oa, w_ob, w_out, ln_g, ln_b)


def _embed_ln_kernel(x_ref, p_ref, wg_ref, wp_ref, bg_ref, g_ref, b_ref, o_ref, *, alpha):
    x = x_ref[...]
    gate = jax.nn.sigmoid(_dot(x.astype(BF16), wg_ref[...]) + bg_ref[...])
    e = gate * _dot(p_ref[...].astype(BF16), wp_ref[...])
    o_ref[...] = _layer_norm(alpha * x + e, g_ref[...], b_ref[...])


def _embed_ln(x, p, pe_w_gate, pe_w_proj, pe_b_gate, ln_g, ln_b, layer, alpha):
    n = x.shape[0]
    tm = min(512, n)
    ln_spec = pl.BlockSpec((None, None, 1, D_MODEL), lambda i: (layer, 3, 0, 0))
    return pl.pallas_call(
        functools.partial(_embed_ln_kernel, alpha=alpha),
        grid=(n // tm,),
        in_specs=[
            pl.BlockSpec((tm, D_MODEL), lambda i: (i, 0)),
            pl.BlockSpec((None, tm, P_DIM), lambda i: (layer, i, 0)),
            pl.BlockSpec((None, D_MODEL, D_MODEL), lambda i: (layer, 0, 0)),
            pl.BlockSpec((None, P_DIM, D_MODEL), lambda i: (layer, 0, 0)),
            pl.BlockSpec((None, 1, D_MODEL), lambda i: (layer, 0, 0)),
            ln_spec, ln_spec,
        ],
        out_specs=pl.BlockSpec((tm, D_MODEL), lambda i: (i, 0)),
        out_shape=jax.ShapeDtypeStruct((n, D_MODEL), F32),
        compiler_params=_cparams(("parallel",)),
        name="embed_ln",
    )(x, p, pe_w_gate, pe_w_proj, pe_b_gate, ln_g, ln_b)


def kernel(x_prompt, x_sample, cache_k, cache_v, state_hgrn, page_table, p_prompt, p_sample, ln_g, ln_b, ffn_w_gate, ffn_w_up, ffn_w_down, w_in, b_merge, sb_bias, hgrn_lb, hgrn_norm_g, w_oa, w_ob, w_out, pe_w_proj, pe_w_gate, pe_b_gate):
    depth = ln_g.shape[0]
    alpha = (2.0 * depth) ** 0.25
    batch, seq, _ = x_prompt.shape
    dec_b, dec_seq, _ = x_sample.shape
    n_pool, page = cache_k.shape[1], cache_k.shape[2]

    lb_all = jnp.cumsum(jax.nn.softmax(hgrn_lb.astype(F32), axis=0), axis=0)
    lb_all = lb_all - lb_all[0:1]

    wg = ffn_w_gate.astype(BF16)
    wu = ffn_w_up.astype(BF16)
    wd = ffn_w_down.astype(BF16)
    w_in_b = w_in.astype(BF16)
    w_kv_t = jnp.swapaxes(w_in_b[:, :, COL:3 * COL], 1, 2)
    w_oa_b = w_oa.astype(BF16)
    w_ob_b = w_ob.astype(BF16)
    w_out_b = w_out.astype(BF16)
    pe_wg_b = pe_w_gate.astype(BF16)
    pe_wp_b = pe_w_proj.astype(BF16)
    ln_g4 = ln_g.reshape(depth, 4, 1, D_MODEL)
    ln_b4 = ln_b.reshape(depth, 4, 1, D_MODEL)
    pe_bg3 = pe_b_gate.reshape(depth, 1, D_MODEL)
    cache_kt = jnp.transpose(cache_k, (0, 1, 3, 4, 2)).reshape(depth, n_pool, WIDTH_A, page)
    cache_vt = jnp.transpose(cache_v, (0, 1, 3, 4, 2)).reshape(depth, n_pool, WIDTH_A, page)
    p_prompt3 = p_prompt.reshape(depth, batch * seq, P_DIM)
    p_sample3 = p_sample.reshape(depth, dec_b * dec_seq, P_DIM)
    state_t = jnp.swapaxes(state_hgrn, -1, -2)
    zero_state = jnp.zeros((batch, N_HEADS_B, DV_B, DK_B), F32)

    def layer_tail(x, p3, i, slots, oa, ob):
        x = _merge_ln(x, oa, ob, slots, b_merge, w_oa_b, w_ob_b, w_out_b, ln_g4, ln_b4, i, alpha)
        x = _ffn_ln(x, wg, wu, wd, ln_g4, ln_b4, i, 1, 2, alpha)
        return _embed_ln(x, p3, pe_wg_b, pe_wp_b, pe_bg3, ln_g4, ln_b4, i, alpha)

    yp = x_prompt.reshape(batch * seq, D_MODEL)
    ys = x_sample.reshape(dec_b * dec_seq, D_MODEL)
    kt_all = vt_all = None
    sp, kd, vd, sd = [], [], [], []
    for i in range(depth):
        lb_row = lb_all[i][None, :]
        ng_row = hgrn_norm_g[i][None, :]
        yp = _ffn_ln(yp, wg, wu, wd, ln_g4, ln_b4, i, 0, 0, alpha)
        slots, kt_all, vt_all = _in_proj_prompt(yp, w_in_b, w_kv_t, i, depth, batch, seq, kt_all, vt_all)
        oa = _attn_prompt(slots, kt_all, vt_all, sb_bias[i], i, batch, seq)
        ob, s_t = _hgrn(slots, lb_row, ng_row, zero_state, batch, seq)
        yp = layer_tail(yp, p_prompt3, i, slots, oa, ob)
        sp.append(jnp.swapaxes(s_t, -1, -2))
        ys = _ffn_ln(ys, wg, wu, wd, ln_g4, ln_b4, i, 0, 0, alpha)
        slots, k_new, v_new = _in_proj_rows(ys, w_in_b, i)
        oa = _attn_decode(slots, k_new, v_new, cache_kt, cache_vt, page_table, sb_bias[i], i,
                          dec_b, dec_seq)
        ob, s_t = _hgrn(slots, lb_row, ng_row, state_t[i], dec_b, dec_seq)
        ys = layer_tail(ys, p_sample3, i, slots, oa, ob)
        kd.append(k_new.reshape(dec_b, dec_seq, N_HEADS_A, HEAD_DIM_A))
        vd.append(v_new.reshape(dec_b, dec_seq, N_HEADS_A, HEAD_DIM_A))
        sd.append(jnp.swapaxes(s_t, -1, -2))

    def token_major(t_all):
        t5 = t_all.reshape(depth, batch, N_HEADS_A, HEAD_DIM_A, seq)
        return jnp.transpose(t5, (0, 1, 4, 2, 3))

    return (yp.reshape(batch, seq, D_MODEL), ys.reshape(dec_b, dec_seq, D_MODEL),
            token_major(kt_all), token_major(vt_all), jnp.stack(sp),
            jnp.stack(kd), jnp.stack(vd), jnp.stack(sd))
```

```python
import functools
import math

import jax
import jax.numpy as jnp
from jax import lax
from jax.experimental import pallas as pl
from jax.experimental.pallas import tpu as pltpu

F32 = jnp.float32
BF16 = jnp.bfloat16

D_MODEL = 1024
N_HEADS_A = 8
HEAD_DIM_A = 64
WIDTH_A = N_HEADS_A * HEAD_DIM_A
N_HEADS_B = 4
DK_B = 128
DV_B = 128
WIDTH_B = N_HEADS_B * DK_B
D_FF = 2816
P_DIM = 256
LN_EPS = 1e-5
RMS_EPS = 1e-6
LOG2E = 1.4426950408889634
COL = 512
N_COLBLK = 11
N_SLOTS = N_COLBLK - 2
(SL_QA, SL_QB, SL_FB, SL_IB, SL_GB, SL_ZA, SL_ZB) = (0, 1, 2, 3, 4, 5, 7)

LANES = 128
SUBLANES = 8
BF16_SUBLANES = 16
MXU_DIM = 256
VMEM_LIMIT_BYTES = 56 * 1024 * 1024
FFN_TILE_F = D_FF // 2
SUBCHUNK_B = 16
PAGES_PER_STEP = 16
ATTN_Q_TILE = 1024
ATTN_ROW_CHUNK = 256
ATTN_KEY_CHUNKS = 2


def _cparams(sem):
    return pltpu.CompilerParams(dimension_semantics=sem, vmem_limit_bytes=VMEM_LIMIT_BYTES)


def _branch_dtype(block_rows):
    return BF16 if block_rows % BF16_SUBLANES == 0 else F32


def _dot(a, b):
    return jnp.dot(a, b, preferred_element_type=F32)


def _dot_nt(a, b):
    return lax.dot_general(a, b, (((1,), (1,)), ((), ())), preferred_element_type=F32)


def _dot_tn(a, b):
    return lax.dot_general(a, b, (((0,), (0,)), ((), ())), preferred_element_type=F32)


def _split2(x):
    hi = x.astype(BF16)
    lo = (x - hi.astype(F32)).astype(BF16)
    return hi, lo


def _split3(x):
    h1 = x.astype(BF16)
    r = x - h1.astype(F32)
    h2 = r.astype(BF16)
    h3 = (r - h2.astype(F32)).astype(BF16)
    return h1, h2, h3


def _softplus(z):
    return jnp.maximum(z, 0.0) + jnp.log(1.0 + jnp.exp(-jnp.abs(z)))


def _softplus2(z2):
    return jnp.maximum(z2, 0.0) + jnp.log(1.0 + jnp.exp2(-jnp.abs(z2))) * LOG2E


def _layer_norm(y, g, b):
    mu = jnp.mean(y, axis=-1, keepdims=True)
    yc = y - mu
    var = jnp.mean(yc * yc, axis=-1, keepdims=True)
    return yc * lax.rsqrt(var + LN_EPS) * g + b


def _suffix_matrix(n, halves):
    row = lax.broadcasted_iota(jnp.int32, (halves * n, n), 0)
    col = lax.broadcasted_iota(jnp.int32, (halves * n, n), 1)
    return jnp.where(jnp.where(row >= n, row - n, row) >= col, 1.0, 0.0).astype(BF16)


def _ffn_ln_kernel(x_ref, wg_ref, wu_ref, wd_ref, g_ref, b_ref, o_ref, acc_ref, *, alpha):
    j = pl.program_id(1)
    x = x_ref[...]
    xb = x.astype(BF16)
    g = _dot(xb, wg_ref[...])
    u = _dot(xb, wu_ref[...])
    h = (g * jax.nn.sigmoid(g) * u).astype(BF16)
    part = _dot(h, wd_ref[...])

    @pl.when(j == 0)
    def _():
        acc_ref[...] = part

    @pl.when(j > 0)
    def _():
        acc_ref[...] += part

    @pl.when(j == pl.num_programs(1) - 1)
    def _():
        y = alpha * x + 0.5 * acc_ref[...]
        o_ref[...] = _layer_norm(y, g_ref[...], b_ref[...])


def _ffn_ln(x, wg, wu, wd, ln_g, ln_b, layer, which, ln_idx, alpha):
    n = x.shape[0]
    tm = min(512, n)
    tf = FFN_TILE_F
    return pl.pallas_call(
        functools.partial(_ffn_ln_kernel, alpha=alpha),
        grid=(n // tm, D_FF // tf),
        in_specs=[
            pl.BlockSpec((tm, D_MODEL), lambda i, j: (i, 0)),
            pl.BlockSpec((None, None, D_MODEL, tf), lambda i, j: (layer, which, 0, j)),
            pl.BlockSpec((None, None, D_MODEL, tf), lambda i, j: (layer, which, 0, j)),
            pl.BlockSpec((None, None, tf, D_MODEL), lambda i, j: (layer, which, j, 0)),
            pl.BlockSpec((None, None, 1, D_MODEL), lambda i, j: (layer, ln_idx, 0, 0)),
            pl.BlockSpec((None, None, 1, D_MODEL), lambda i, j: (layer, ln_idx, 0, 0)),
        ],
        out_specs=pl.BlockSpec((tm, D_MODEL), lambda i, j: (i, 0)),
        out_shape=jax.ShapeDtypeStruct((n, D_MODEL), F32),
        scratch_shapes=[pltpu.VMEM((tm, D_MODEL), F32)],
        compiler_params=_cparams(("parallel", "arbitrary")),
        name="ffn_ln",
    )(x, wg, wu, wd, ln_g, ln_b)


def _slot_of_colblk(j):
    return jnp.where(j < 3, 0, j - 2)


def _in_proj_kernel(x_ref, w_ref, *rest, feature_major, n_alias):
    if feature_major:
        wkv_ref = rest[0]
        rest = rest[1:]
    o_ref, k_ref, v_ref, xb_scr = rest[n_alias:]
    j = pl.program_id(1)

    @pl.when(j == 0)
    def _():
        xb_scr[...] = x_ref[...].astype(BF16)

    def kv():
        if feature_major:
            return _dot_nt(wkv_ref[...], xb_scr[...])
        return _dot(xb_scr[...], w_ref[...])

    @pl.when(j == 1)
    def _():
        k_ref[...] = kv()

    @pl.when(j == 2)
    def _():
        v_ref[...] = kv()

    @pl.when((j == 0) | (j > 2))
    def _():
        o_ref[...] = _dot(xb_scr[...], w_ref[...])


def _in_proj_rows(x, w_in, layer):
    n = x.shape[0]
    tm = min(1024, n)
    kv_spec = pl.BlockSpec((tm, COL), lambda i, j: (i, 0))
    return pl.pallas_call(
        functools.partial(_in_proj_kernel, feature_major=False, n_alias=0),
        grid=(n // tm, N_COLBLK),
        in_specs=[
            pl.BlockSpec((tm, D_MODEL), lambda i, j: (i, 0)),
            pl.BlockSpec((None, D_MODEL, COL), lambda i, j: (layer, 0, j)),
        ],
        out_specs=[
            pl.BlockSpec((None, tm, COL), lambda i, j: (_slot_of_colblk(j), i, 0)),
            kv_spec, kv_spec,
        ],
        out_shape=[
            jax.ShapeDtypeStruct((N_SLOTS, n, COL), F32),
            jax.ShapeDtypeStruct((n, COL), F32),
            jax.ShapeDtypeStruct((n, COL), F32),
        ],
        scratch_shapes=[pltpu.VMEM((tm, D_MODEL), BF16)],
        compiler_params=_cparams(("parallel", "arbitrary")),
        name="in_proj_rows",
    )(x, w_in)


def _in_proj_prompt(x, w_in, w_kv_t, layer, depth, batch, seq, kt_prev, vt_prev):
    n = x.shape[0]
    tm = min(2048, seq)
    nt = seq // tm
    n_alias = 0 if kt_prev is None else 2
    kv_spec = pl.BlockSpec((None, None, COL, tm), lambda i, j: (layer, i // nt, 0, i % nt))
    in_specs = [
        pl.BlockSpec((tm, D_MODEL), lambda i, j: (i, 0)),
        pl.BlockSpec((None, D_MODEL, COL), lambda i, j: (layer, 0, j)),
        pl.BlockSpec((None, COL, D_MODEL), lambda i, j: (layer, jnp.clip(j - 1, 0, 1), 0)),
    ]
    args = [x, w_in, w_kv_t]
    aliases = {}
    if n_alias:
        in_specs += [pl.BlockSpec(memory_space=pl.ANY)] * 2
        args += [kt_prev, vt_prev]
        aliases = {3: 1, 4: 2}
    kv_shape = jax.ShapeDtypeStruct((depth, batch, COL, seq), F32)
    return pl.pallas_call(
        functools.partial(_in_proj_kernel, feature_major=True, n_alias=n_alias),
        grid=(n // tm, N_COLBLK),
        in_specs=in_specs,
        out_specs=[
            pl.BlockSpec((None, tm, COL), lambda i, j: (_slot_of_colblk(j), i, 0)),
            kv_spec, kv_spec,
        ],
        out_shape=[jax.ShapeDtypeStruct((N_SLOTS, n, COL), F32), kv_shape, kv_shape],
        scratch_shapes=[pltpu.VMEM((tm, D_MODEL), BF16)],
        input_output_aliases=aliases,
        compiler_params=_cparams(("parallel", "arbitrary")),
        name="in_proj_prompt",
    )(*args)


def _sb_suffix(zs, suffix_m, causals):
    blk = suffix_m.shape[1]
    nblk = zs[0].shape[1] // blk
    nls = [_softplus2(z) for z in zs]
    nls = [nl if m is None else jnp.where(m, nl, 0.0) for nl, m in zip(nls, causals)]

    def stacked_split(nl):
        halves = _split2(nl) if suffix_m.shape[0] == 2 * blk else (nl.astype(BF16),)
        return jnp.concatenate(
            [jnp.concatenate([h[:, j * blk:(j + 1) * blk] for h in halves], axis=1)
             for j in range(nblk)], axis=0)

    return nls, [_dot(stacked_split(nl), suffix_m) for nl in nls]


def _sb_finish(zs, nls, suffix, cs, causals, chained=False):
    blk = suffix[0].shape[1]
    nblk = zs[0].shape[1] // blk
    rows = zs[0].shape[0]
    out_a, out_c = [], []
    c = None
    for z, nl, suf, c_in, m in zip(zs, nls, suffix, cs, causals):
        c = c_in if (c is None or not chained) else c
        parts = [None] * nblk
        for j in reversed(range(nblk)):
            sl = slice(j * blk, (j + 1) * blk)
            parts[j] = jnp.exp2(z[:, sl] - suf[j * rows:(j + 1) * rows, :] - c)
            c = c + jnp.sum(nl[:, sl], axis=-1, keepdims=True)
        a = parts[0] if nblk == 1 else jnp.concatenate(parts, axis=1)
        out_a.append(a if m is None else jnp.where(m, a, 0.0))
        out_c.append(c)
    return out_a, out_c


def _attn_prompt_kernel(bias_ref, q_ref, kt_ref, vt_ref, o_ref, *, tile, row_chunk, key_chunks):
    pair = pl.program_id(1)
    qi = pl.program_id(2)
    q = q_ref[...] * (LOG2E / math.sqrt(HEAD_DIM_A))
    lane = lax.broadcasted_iota(jnp.int32, (tile, LANES), 1)
    q0 = jnp.where(lane < HEAD_DIM_A, q, 0.0)
    q_heads = (q0.astype(BF16), (q - q0).astype(BF16))
    biases = (bias_ref[2 * pair] * LOG2E, bias_ref[2 * pair + 1] * LOG2E)
    rc = min(row_chunk, tile)
    nrc = tile // rc
    kfac = math.gcd(key_chunks, nrc)
    kb = kfac * rc
    chains = [(hh, r) for hh in range(2) for r in range(nrc)]
    q_chain = [q_heads[hh][r * rc:(r + 1) * rc] for hh, r in chains]
    row = lax.broadcasted_iota(jnp.int32, (rc, kb), 0)
    col = lax.broadcasted_iota(jnp.int32, (rc, kb), 1)
    suffix_m = _suffix_matrix(MXU_DIM if rc % MXU_DIM == 0 else rc, 1)
    feat = lax.broadcasted_iota(jnp.int32, (LANES, kb), 0)
    head0_feat = feat < HEAD_DIM_A

    def step(kj, cs, ots, diag):
        start = pl.multiple_of(kj * kb, kb)
        kt = kt_ref[:, pl.ds(start, kb)].astype(BF16)
        vt = vt_ref[:, pl.ds(start, kb)]
        vt_heads = (jnp.where(head0_feat, vt, 0.0).astype(BF16),
                    jnp.where(head0_feat, 0.0, vt).astype(BF16))
        first = 0 if diag is None else diag * kfac
        live = [i for i, (_, r) in enumerate(chains) if r >= first]
        masks = [col < row + (chains[i][1] - first) * rc
                 if diag is not None and chains[i][1] < first + kfac else None for i in live]
        zs = [_dot(q_chain[i], kt) + biases[chains[i][0]] for i in live]
        nls, suffix = _sb_suffix(zs, suffix_m, masks)
        a_s, c_new = _sb_finish(zs, nls, suffix, [cs[i] for i in live], masks)
        cs, ots = list(cs), list(ots)
        for i, a, c in zip(live, a_s, c_new):
            hh, r = chains[i]
            cs[i] = c
            ots[r] = ots[r] + _dot_nt(vt_heads[hh], a.astype(BF16))
        return tuple(cs), tuple(ots)

    cs = tuple(jnp.zeros((rc, 1), F32) for _ in chains)
    ots = tuple(jnp.zeros((LANES, rc), F32) for _ in range(nrc))
    n_diag = nrc // kfac
    for d in reversed(range(n_diag)):
        cs, ots = step(qi * n_diag + d, cs, ots, d)
    cs, ots = lax.fori_loop(0, qi * n_diag,
                            lambda jj, co: step(qi * n_diag - 1 - jj, co[0], co[1], None), (cs, ots))
    for r in range(nrc):
        o_ref[r * rc:(r + 1) * rc, :] = ots[r].T.astype(o_ref.dtype)


def _attn_prompt(slots, kt_all, vt_all, sb_bias, layer, batch, seq):
    tile = min(ATTN_Q_TILE, seq)
    nq = seq // tile
    n = batch * seq
    kv_spec = pl.BlockSpec((None, None, LANES, seq), lambda b, p, i: (layer, b, p, 0))
    return pl.pallas_call(
        functools.partial(_attn_prompt_kernel, tile=tile, row_chunk=ATTN_ROW_CHUNK,
                          key_chunks=ATTN_KEY_CHUNKS),
        grid=(batch, WIDTH_A // LANES, nq),
        in_specs=[
            pl.BlockSpec(memory_space=pltpu.SMEM),
            pl.BlockSpec((None, tile, LANES), lambda b, p, i: (SL_QA, b * nq + i, p)),
            kv_spec, kv_spec,
        ],
        out_specs=pl.BlockSpec((tile, LANES), lambda b, p, i: (b * nq + i, p)),
        out_shape=jax.ShapeDtypeStruct((n, WIDTH_A), _branch_dtype(tile)),
        compiler_params=_cparams(("parallel", "parallel", "arbitrary")),
        name="sb_attn_prompt",
    )(sb_bias, slots, kt_all, vt_all)


def _attn_decode_kernel(pt_ref, bias_ref, q_ref, kn_ref, vn_ref, *rest, n_new, page, pps):
    k_refs = rest[:pps]
    v_refs = rest[pps:2 * pps]
    o_ref = rest[2 * pps]
    qbd_scr, c_scr, acc_scr, kpad_scr, vpad_scr = rest[2 * pps + 1:]
    g = pl.program_id(1)
    rows = N_HEADS_A * n_new
    bias = bias_ref[...] * LOG2E

    @pl.when(g == 0)
    def _():
        q = q_ref[...] * (LOG2E / math.sqrt(HEAD_DIM_A))
        qt = jnp.concatenate([q] * N_HEADS_A, axis=0)
        r = lax.broadcasted_iota(jnp.int32, (rows, WIDTH_A), 0)
        f = lax.broadcasted_iota(jnp.int32, (rows, WIDTH_A), 1)
        qbd = jnp.where(r // n_new == f // HEAD_DIM_A, qt, 0.0).astype(BF16)
        qbd_scr[...] = qbd
        kpad_scr[...] = jnp.zeros_like(kpad_scr)
        vpad_scr[...] = jnp.zeros_like(vpad_scr)
        kpad_scr[0:n_new, :] = kn_ref[...]
        vpad_scr[0:n_new, :] = vn_ref[...]
        rr = lax.broadcasted_iota(jnp.int32, (rows, page), 0)
        ss = lax.broadcasted_iota(jnp.int32, (rows, page), 1)
        causal = ss < (rr % n_new)
        z = _dot_nt(qbd, kpad_scr[...].astype(BF16)) + bias
        nls, suffix = _sb_suffix([z], _suffix_matrix(page, 2), [causal])
        (a,), (c,) = _sb_finish([z], nls, suffix, [jnp.zeros((rows, 1), F32)], [causal])
        acc_scr[...] = _dot(a.astype(BF16), vpad_scr[...].astype(BF16))
        c_scr[...] = c

    kt = jnp.concatenate([r[...] for r in k_refs], axis=1).astype(BF16)
    vt = jnp.concatenate([r[...] for r in v_refs], axis=1).astype(BF16)
    z = _dot(qbd_scr[...], kt) + bias
    blk = math.gcd(MXU_DIM, pps * page)
    nls, suffix = _sb_suffix([z], _suffix_matrix(blk, 2), [None])
    (a,), (c,) = _sb_finish([z], nls, suffix, [c_scr[...]], [None])
    acc_scr[...] += _dot_nt(a.astype(BF16), vt)
    c_scr[...] = c

    @pl.when(g == pl.num_programs(1) - 1)
    def _():
        acc = acc_scr[...]
        f = lax.broadcasted_iota(jnp.int32, (n_new, WIDTH_A), 1)
        out = jnp.zeros((n_new, WIDTH_A), F32)
        for h in range(N_HEADS_A):
            out = out + jnp.where(f // HEAD_DIM_A == h, acc[h * n_new:(h + 1) * n_new, :], 0.0)
        o_ref[...] = out


def _attn_decode(slots, k_new, v_new, cache_kt, cache_vt, page_table, sb_bias, layer, dec_b, n_new):
    page = cache_kt.shape[3]
    n_pages = page_table.shape[1]
    pps = math.gcd(PAGES_PER_STEP, n_pages)
    steps = n_pages // pps
    rows = N_HEADS_A * n_new
    bias_rows = jnp.repeat(sb_bias.astype(F32), n_new)[:, None]

    def page_spec(i):
        return pl.BlockSpec(
            (None, None, WIDTH_A, page),
            lambda b, g, pt: (layer, pt[b, n_pages - (g + 1) * pps + i], 0, 0))

    new_spec = pl.BlockSpec((n_new, COL), lambda b, g, pt: (b, 0))
    grid_spec = pltpu.PrefetchScalarGridSpec(
        num_scalar_prefetch=1,
        grid=(dec_b, steps),
        in_specs=[
            pl.BlockSpec((rows, 1), lambda b, g, pt: (0, 0)),
            pl.BlockSpec((None, n_new, COL), lambda b, g, pt: (SL_QA, b, 0)),
            new_spec, new_spec,
        ] + [page_spec(i) for i in range(pps)] + [page_spec(i) for i in range(pps)],
        out_specs=pl.BlockSpec((n_new, WIDTH_A), lambda b, g, pt: (b, 0)),
        scratch_shapes=[
            pltpu.VMEM((rows, WIDTH_A), BF16),
            pltpu.VMEM((rows, 1), F32),
            pltpu.VMEM((rows, WIDTH_A), F32),
            pltpu.VMEM((page, WIDTH_A), F32),
            pltpu.VMEM((page, WIDTH_A), F32),
        ],
    )
    return pl.pallas_call(
        functools.partial(_attn_decode_kernel, n_new=n_new, page=page, pps=pps),
        grid_spec=grid_spec,
        out_shape=jax.ShapeDtypeStruct((dec_b * n_new, WIDTH_A), F32),
        compiler_params=_cparams(("parallel", "arbitrary")),
        name="sb_attn_decode",
    )(page_table, bias_rows, slots, k_new, v_new, *([cache_kt] * pps), *([cache_vt] * pps))


def _hgrn_kernel(lb_ref, ng_ref, q_ref, f_ref, i_ref, g_ref, s0_ref, o_ref, s_ref,
                 st_scr, b_scr, q_scr, k_scr, o_scr, *, chunk, sub):
    c = pl.program_id(1)

    @pl.when(c == 0)
    def _():
        st_scr[...] = s0_ref[...]

    lb = lb_ref[...]
    zf = f_ref[...]
    a1 = jnp.log(lb)
    a2 = jnp.log1p(-lb) - _softplus(-zf)
    logf = jnp.maximum(a1, a2) + jnp.log(1.0 + jnp.exp(-jnp.abs(a1 - a2)))
    k_scr[...] = (1.0 - lb) * jax.nn.sigmoid(-zf)
    qv = q_ref[...]
    q_scr[...] = qv * jax.nn.sigmoid(qv)
    r = lax.broadcasted_iota(jnp.int32, (chunk, chunk), 0)
    s = lax.broadcasted_iota(jnp.int32, (chunk, chunk), 1)
    lmat = jnp.where((s <= r) & (s // sub == r // sub), 1.0, 0.0).astype(BF16)
    h1, h2, h3 = _split3(logf)
    b_scr[...] = (_dot(lmat, h1) + _dot(lmat, h2) + _dot(lmat, h3)) * LOG2E

    grp = math.gcd(sub, SUBLANES)
    ngrp = sub // grp
    gidx = lax.broadcasted_iota(jnp.int32, (grp, 1), 0)

    def sub_step(ci, carry):
        r0 = pl.multiple_of(ci * sub, sub)
        heads = []
        for h in range(N_HEADS_B):
            cols = slice(h * DK_B, (h + 1) * DK_B)
            b = b_scr[pl.ds(r0, sub), cols]
            qs = q_scr[pl.ds(r0, sub), cols]
            kk = k_scr[pl.ds(r0, sub), cols]
            vv = i_ref[pl.ds(r0, sub), cols]
            b_last = b[sub - 1:sub, :]
            st = st_scr[h]
            o_state = _dot_nt((qs * jnp.exp2(b)).astype(BF16), st.astype(BF16))
            ke = kk * jnp.exp2(b_last - b)
            st_new = st * jnp.exp2(b_last) + _dot_tn(vv.astype(BF16), ke.astype(BF16))
            heads.append((cols, b, qs, kk, vv, o_state, st_new))
        for h, (cols, b, qs, kk, vv, o_state, st_new) in enumerate(heads):
            og = [None] * ngrp
            bg = [b[g * grp:(g + 1) * grp] for g in range(ngrp)]
            qg = [qs[g * grp:(g + 1) * grp] for g in range(ngrp)]
            for t in range(sub):
                bt = b[t:t + 1, :]
                kt = kk[t:t + 1, :]
                vt = vv[t:t + 1, :]
                for g in range(t // grp, ngrp):
                    if g == t // grp:
                        w = jnp.exp2(jnp.minimum(bg[g] - bt, 0.0))
                        p = jnp.sum(qg[g] * kt * w, axis=-1, keepdims=True)
                        p = jnp.where(gidx >= t - g * grp, p, 0.0)
                    else:
                        p = jnp.sum(qg[g] * kt * jnp.exp2(bg[g] - bt), axis=-1, keepdims=True)
                    og[g] = p * vt if og[g] is None else og[g] + p * vt
            o_scr[pl.ds(r0, sub), cols] = (og[0] if ngrp == 1 else jnp.concatenate(og, axis=0)) + o_state
            st_scr[h] = st_new
        return carry

    lax.fori_loop(0, chunk // sub, sub_step, 0)

    gate = g_ref[...]
    gate = gate * jax.nn.sigmoid(gate)
    ng = ng_ref[...]
    for h in range(N_HEADS_B):
        cols = slice(h * DV_B, (h + 1) * DV_B)
        o = o_scr[:, cols]
        o = o * lax.rsqrt(jnp.mean(o * o, axis=-1, keepdims=True) + RMS_EPS) * ng
        o_ref[:, cols] = (o * gate[:, cols]).astype(o_ref.dtype)

    @pl.when(c == pl.num_programs(1) - 1)
    def _():
        s_ref[...] = st_scr[...]


def _hgrn(slots, lb_row, norm_g_row, s0_t, batch, seq):
    chunk = min(128, seq)
    sub = min(SUBCHUNK_B, chunk)
    nc = seq // chunk
    n = batch * seq
    blk = lambda sl: pl.BlockSpec((None, chunk, COL), lambda b, c: (sl, b * nc + c, 0))
    st_spec = pl.BlockSpec((None, N_HEADS_B, DV_B, DK_B), lambda b, c: (b, 0, 0, 0))
    return pl.pallas_call(
        functools.partial(_hgrn_kernel, chunk=chunk, sub=sub),
        grid=(batch, nc),
        in_specs=[
            pl.BlockSpec((1, WIDTH_B), lambda b, c: (0, 0)),
            pl.BlockSpec((1, DV_B), lambda b, c: (0, 0)),
            blk(SL_QB), blk(SL_FB), blk(SL_IB), blk(SL_GB),
            st_spec,
        ],
        out_specs=[
            pl.BlockSpec((chunk, COL), lambda b, c: (b * nc + c, 0)),
            st_spec,
        ],
        out_shape=[
            jax.ShapeDtypeStruct((n, COL), _branch_dtype(chunk)),
            jax.ShapeDtypeStruct((batch, N_HEADS_B, DV_B, DK_B), F32),
        ],
        scratch_shapes=[
            pltpu.VMEM((N_HEADS_B, DV_B, DK_B), F32),
            pltpu.VMEM((chunk, WIDTH_B), F32),
            pltpu.VMEM((chunk, WIDTH_B), F32),
            pltpu.VMEM((chunk, WIDTH_B), F32),
            pltpu.VMEM((chunk, COL), F32),
        ],
        compiler_params=_cparams(("parallel", "arbitrary")),
        name="hgrn2",
    )(lb_row, norm_g_row, slots, slots, slots, slots, s0_t)


def _merge_ln_kernel(x_ref, oa_ref, ob_ref, za0_ref, za1_ref, zb0_ref, zb1_ref, bm_ref,
                     woa_ref, wob_ref, wout_ref, g_ref, b_ref, o_ref, *, alpha):
    ya = _dot(oa_ref[...].astype(BF16), woa_ref[...])
    yb = _dot(ob_ref[...].astype(BF16), wob_ref[...])
    za = jnp.concatenate([za0_ref[...], za1_ref[...]], axis=-1)
    zb = jnp.concatenate([zb0_ref[...], zb1_ref[...]], axis=-1)
    bm = bm_ref[...]
    m = jax.nn.sigmoid(za + bm[0:1, :]) * ya + jax.nn.sigmoid(zb + bm[1:2, :]) * yb
    mix = _dot(m.astype(BF16), wout_ref[...])
    o_ref[...] = _layer_norm(alpha * x_ref[...] + mix, g_ref[...], b_ref[...])


def _merge_ln(x, oa, ob, slots, b_merge, w_oa, w_ob, w_out, ln_g, ln_b, layer, alpha):
    n = x.shape[0]
    tm = min(512, n)
    row = lambda w: pl.BlockSpec((tm, w), lambda i: (i, 0))
    zblk = lambda sl: pl.BlockSpec((None, tm, COL), lambda i: (sl, i, 0))
    wspec = lambda k: pl.BlockSpec((None, k, D_MODEL), lambda i: (layer, 0, 0))
    ln_spec = pl.BlockSpec((None, None, 1, D_MODEL), lambda i: (layer, 1, 0, 0))
    return pl.pallas_call(
        functools.partial(_merge_ln_kernel, alpha=alpha),
        grid=(n // tm,),
        in_specs=[
            row(D_MODEL), row(WIDTH_A), row(COL),
            zblk(SL_ZA), zblk(SL_ZA + 1), zblk(SL_ZB), zblk(SL_ZB + 1),
            pl.BlockSpec((None, 2, D_MODEL), lambda i: (layer, 0, 0)),
            wspec(WIDTH_A), wspec(COL), wspec(D_MODEL),
            ln_spec, ln_spec,
        ],
        out_specs=row(D_MODEL),
        out_shape=jax.ShapeDtypeStruct((n, D_MODEL), F32),
        compiler_params=_cparams(("parallel",)),
        name="merge_ln",
    )(x, oa, ob, slots, slots, slots, slots, b_merge, w_oa, w_ob, w_out, ln_g, ln_b)


def _embed_ln_kernel(x_ref, p_ref, wg_ref, wp_ref, bg_ref, g_ref, b_ref, o_ref, *, alpha):
    x = x_ref[...]
    gate = jax.nn.sigmoid(_dot(x.astype(BF16), wg_ref[...]) + bg_ref[...])
    e = gate * _dot(p_ref[...].astype(BF16), wp_ref[...])
    o_ref[...] = _layer_norm(alpha * x + e, g_ref[...], b_ref[...])


def _embed_ln(x, p, pe_w_gate, pe_w_proj, pe_b_gate, ln_g, ln_b, layer, alpha):
    n = x.shape[0]
    tm = min(512, n)
    ln_spec = pl.BlockSpec((None, None, 1, D_MODEL), lambda i: (layer, 3, 0, 0))
    return pl.pallas_call(
        functools.partial(_embed_ln_kernel, alpha=alpha),
        grid=(n // tm,),
        in_specs=[
            pl.BlockSpec((tm, D_MODEL), lambda i: (i, 0)),
            pl.BlockSpec((None, tm, P_DIM), lambda i: (layer, i, 0)),
            pl.BlockSpec((None, D_MODEL, D_MODEL), lambda i: (layer, 0, 0)),
            pl.BlockSpec((None, P_DIM, D_MODEL), lambda i: (layer, 0, 0)),
            pl.BlockSpec((None, 1, D_MODEL), lambda i: (layer, 0, 0)),
            ln_spec, ln_spec,
        ],
        out_specs=pl.BlockSpec((tm, D_MODEL), lambda i: (i, 0)),
        out_shape=jax.ShapeDtypeStruct((n, D_MODEL), F32),
        compiler_params=_cparams(("parallel",)),
        name="embed_ln",
    )(x, p, pe_w_gate, pe_w_proj, pe_b_gate, ln_g, ln_b)


def kernel(x_prompt, x_sample, cache_k, cache_v, state_hgrn, page_table, p_prompt, p_sample, ln_g, ln_b, ffn_w_gate, ffn_w_up, ffn_w_down, w_in, b_merge, sb_bias, hgrn_lb, hgrn_norm_g, w_oa, w_ob, w_out, pe_w_proj, pe_w_gate, pe_b_gate):
    depth = ln_g.shape[0]
    alpha = (2.0 * depth) ** 0.25
    batch, seq, _ = x_prompt.shape
    dec_b, dec_seq, _ = x_sample.shape
    n_pool, page = cache_k.shape[1], cache_k.shape[2]

    lb_all = jnp.cumsum(jax.nn.softmax(hgrn_lb.astype(F32), axis=0), axis=0)
    lb_all = lb_all - lb_all[0:1]

    wg = ffn_w_gate.astype(BF16)
    wu = ffn_w_up.astype(BF16)
    wd = ffn_w_down.astype(BF16)
    w_in_b = w_in.astype(BF16)
    w_kv_t = jnp.swapaxes(w_in_b[:, :, COL:3 * COL], 1, 2)
    w_oa_b = w_oa.astype(BF16)
    w_ob_b = w_ob.astype(BF16)
    w_out_b = w_out.astype(BF16)
    pe_wg_b = pe_w_gate.astype(BF16)
    pe_wp_b = pe_w_proj.astype(BF16)
    ln_g4 = ln_g.reshape(depth, 4, 1, D_MODEL)
    ln_b4 = ln_b.reshape(depth, 4, 1, D_MODEL)
    pe_bg3 = pe_b_gate.reshape(depth, 1, D_MODEL)
    cache_kt = jnp.transpose(cache_k, (0, 1, 3, 4, 2)).reshape(depth, n_pool, WIDTH_A, page)
    cache_vt = jnp.transpose(cache_v, (0, 1, 3, 4, 2)).reshape(depth, n_pool, WIDTH_A, page)
    p_prompt3 = p_prompt.reshape(depth, batch * seq, P_DIM)
    p_sample3 = p_sample.reshape(depth, dec_b * dec_seq, P_DIM)
    state_t = jnp.swapaxes(state_hgrn, -1, -2)
    zero_state = jnp.zeros((batch, N_HEADS_B, DV_B, DK_B), F32)

    def layer_tail(x, p3, i, slots, oa, ob):
        x = _merge_ln(x, oa, ob, slots, b_merge, w_oa_b, w_ob_b, w_out_b, ln_g4, ln_b4, i, alpha)
        x = _ffn_ln(x, wg, wu, wd, ln_g4, ln_b4, i, 1, 2, alpha)
        return _embed_ln(x, p3, pe_wg_b, pe_wp_b, pe_bg3, ln_g4, ln_b4, i, alpha)

    yp = x_prompt.reshape(batch * seq, D_MODEL)
    ys = x_sample.reshape(dec_b * dec_seq, D_MODEL)
    kt_all = vt_all = None
    sp, kd, vd, sd = [], [], [], []
    for i in range(depth):
        lb_row = lb_all[i][None, :]
        ng_row = hgrn_norm_g[i][None, :]
        yp = _ffn_ln(yp, wg, wu, wd, ln_g4, ln_b4, i, 0, 0, alpha)
        slots, kt_all, vt_all = _in_proj_prompt(yp, w_in_b, w_kv_t, i, depth, batch, seq, kt_all, vt_all)
        oa = _attn_prompt(slots, kt_all, vt_all, sb_bias[i], i, batch, seq)
        ob, s_t = _hgrn(slots, lb_row, ng_row, zero_state, batch, seq)
        yp = layer_tail(yp, p_prompt3, i, slots, oa, ob)
        sp.append(jnp.swapaxes(s_t, -1, -2))
        ys = _ffn_ln(ys, wg, wu, wd, ln_g4, ln_b4, i, 0, 0, alpha)
        slots, k_new, v_new = _in_proj_rows(ys, w_in_b, i)
        oa = _attn_decode(slots, k_new, v_new, cache_kt, cache_vt, page_table, sb_bias[i], i,
                          dec_b, dec_seq)
        ob, s_t = _hgrn(slots, lb_row, ng_row, state_t[i], dec_b, dec_seq)
        ys = layer_tail(ys, p_sample3, i, slots, oa, ob)
        kd.append(k_new.reshape(dec_b, dec_seq, N_HEADS_A, HEAD_DIM_A))
        vd.append(v_new.reshape(dec_b, dec_seq, N_HEADS_A, HEAD_DIM_A))
        sd.append(jnp.swapaxes(s_t, -1, -2))

    def token_major(t_all):
        t5 = t_all.reshape(depth, batch, N_HEADS_A, HEAD_DIM_A, seq)
        return jnp.transpose(t5, (0, 1, 4, 2, 3))

    return (yp.reshape(batch, seq, D_MODEL), ys.reshape(dec_b, dec_seq, D_MODEL),
            token_major(kt_all), token_major(vt_all), jnp.stack(sp),
            jnp.stack(kd), jnp.stack(vd), jnp.stack(sd))
```

```python
import functools
import math

import jax
import jax.numpy as jnp
from jax import lax
from jax.experimental import pallas as pl
from jax.experimental.pallas import tpu as pltpu

F32 = jnp.float32
BF16 = jnp.bfloat16

D_MODEL = 1024
N_HEADS_A = 8
HEAD_DIM_A = 64
WIDTH_A = N_HEADS_A * HEAD_DIM_A
N_HEADS_B = 4
DK_B = 128
DV_B = 128
WIDTH_B = N_HEADS_B * DK_B
D_FF = 2816
P_DIM = 256
LN_EPS = 1e-5
RMS_EPS = 1e-6
LOG2E = 1.4426950408889634
COL = 512
N_COLBLK = 11
N_SLOTS = N_COLBLK - 2
(SL_QA, SL_QB, SL_FB, SL_IB, SL_GB, SL_ZA, SL_ZB) = (0, 1, 2, 3, 4, 5, 7)

LANES = 128
SUBLANES = 8
BF16_SUBLANES = 16
MXU_DIM = 256
VMEM_LIMIT_BYTES = 56 * 1024 * 1024
FFN_TILE_F = D_FF // 2
SUBCHUNK_B = 16
PAGES_PER_STEP = 32
ATTN_Q_TILE = 1024
ATTN_ROW_CHUNK = 256
ATTN_KEY_CHUNKS = 2


def _cparams(sem):
    return pltpu.CompilerParams(dimension_semantics=sem, vmem_limit_bytes=VMEM_LIMIT_BYTES)


def _branch_dtype(block_rows):
    return BF16 if block_rows % BF16_SUBLANES == 0 else F32


def _dot(a, b):
    return jnp.dot(a, b, preferred_element_type=F32)


def _dot_nt(a, b):
    return lax.dot_general(a, b, (((1,), (1,)), ((), ())), preferred_element_type=F32)


def _dot_tn(a, b):
    return lax.dot_general(a, b, (((0,), (0,)), ((), ())), preferred_element_type=F32)


def _split2(x):
    hi = x.astype(BF16)
    lo = (x - hi.astype(F32)).astype(BF16)
    return hi, lo


def _split3(x):
    h1 = x.astype(BF16)
    r = x - h1.astype(F32)
    h2 = r.astype(BF16)
    h3 = (r - h2.astype(F32)).astype(BF16)
    return h1, h2, h3


def _softplus(z):
    return jnp.maximum(z, 0.0) + jnp.log(1.0 + jnp.exp(-jnp.abs(z)))


def _softplus2(z2):
    return jnp.maximum(z2, 0.0) + jnp.log(1.0 + jnp.exp2(-jnp.abs(z2))) * LOG2E


def _layer_norm(y, g, b):
    mu = jnp.mean(y, axis=-1, keepdims=True)
    yc = y - mu
    var = jnp.mean(yc * yc, axis=-1, keepdims=True)
    return yc * lax.rsqrt(var + LN_EPS) * g + b


def _suffix_matrix(n, halves):
    row = lax.broadcasted_iota(jnp.int32, (halves * n, n), 0)
    col = lax.broadcasted_iota(jnp.int32, (halves * n, n), 1)
    return jnp.where(jnp.where(row >= n, row - n, row) >= col, 1.0, 0.0).astype(BF16)


def _ffn_ln_kernel(x_ref, wg_ref, wu_ref, wd_ref, g_ref, b_ref, o_ref, acc_ref, *, alpha):
    j = pl.program_id(1)
    x = x_ref[...]
    xb = x.astype(BF16)
    g = _dot(xb, wg_ref[...])
    u = _dot(xb, wu_ref[...])
    h = (g * jax.nn.sigmoid(g) * u).astype(BF16)
    part = _dot(h, wd_ref[...])

    @pl.when(j == 0)
    def _():
        acc_ref[...] = part

    @pl.when(j > 0)
    def _():
        acc_ref[...] += part

    @pl.when(j == pl.num_programs(1) - 1)
    def _():
        y = alpha * x + 0.5 * acc_ref[...]
        o_ref[...] = _layer_norm(y, g_ref[...], b_ref[...])


def _ffn_ln(x, wg, wu, wd, ln_g, ln_b, layer, which, ln_idx, alpha):
    n = x.shape[0]
    tm = min(512, n)
    tf = FFN_TILE_F
    return pl.pallas_call(
        functools.partial(_ffn_ln_kernel, alpha=alpha),
        grid=(n // tm, D_FF // tf),
        in_specs=[
            pl.BlockSpec((tm, D_MODEL), lambda i, j: (i, 0)),
            pl.BlockSpec((None, None, D_MODEL, tf), lambda i, j: (layer, which, 0, j)),
            pl.BlockSpec((None, None, D_MODEL, tf), lambda i, j: (layer, which, 0, j)),
            pl.BlockSpec((None, None, tf, D_MODEL), lambda i, j: (layer, which, j, 0)),
            pl.BlockSpec((None, None, 1, D_MODEL), lambda i, j: (layer, ln_idx, 0, 0)),
            pl.BlockSpec((None, None, 1, D_MODEL), lambda i, j: (layer, ln_idx, 0, 0)),
        ],
        out_specs=pl.BlockSpec((tm, D_MODEL), lambda i, j: (i, 0)),
        out_shape=jax.ShapeDtypeStruct((n, D_MODEL), F32),
        scratch_shapes=[pltpu.VMEM((tm, D_MODEL), F32)],
        compiler_params=_cparams(("parallel", "arbitrary")),
        name="ffn_ln",
    )(x, wg, wu, wd, ln_g, ln_b)


def _slot_of_colblk(j):
    return jnp.where(j < 3, 0, j - 2)


def _in_proj_kernel(x_ref, w_ref, *rest, feature_major, n_alias):
    if feature_major:
        wkv_ref = rest[0]
        rest = rest[1:]
    o_ref, k_ref, v_ref, xb_scr = rest[n_alias:]
    j = pl.program_id(1)

    @pl.when(j == 0)
    def _():
        xb_scr[...] = x_ref[...].astype(BF16)

    def kv():
        if feature_major:
            return _dot_nt(wkv_ref[...], xb_scr[...])
        return _dot(xb_scr[...], w_ref[...])

    @pl.when(j == 1)
    def _():
        k_ref[...] = kv()

    @pl.when(j == 2)
    def _():
        v_ref[...] = kv()

    @pl.when((j == 0) | (j > 2))
    def _():
        o_ref[...] = _dot(xb_scr[...], w_ref[...])


def _in_proj_rows(x, w_in, layer):
    n = x.shape[0]
    tm = min(1024, n)
    kv_spec = pl.BlockSpec((tm, COL), lambda i, j: (i, 0))
    return pl.pallas_call(
        functools.partial(_in_proj_kernel, feature_major=False, n_alias=0),
        grid=(n // tm, N_COLBLK),
        in_specs=[
            pl.BlockSpec((tm, D_MODEL), lambda i, j: (i, 0)),
            pl.BlockSpec((None, D_MODEL, COL), lambda i, j: (layer, 0, j)),
        ],
        out_specs=[
            pl.BlockSpec((None, tm, COL), lambda i, j: (_slot_of_colblk(j), i, 0)),
            kv_spec, kv_spec,
        ],
        out_shape=[
            jax.ShapeDtypeStruct((N_SLOTS, n, COL), F32),
            jax.ShapeDtypeStruct((n, COL), F32),
            jax.ShapeDtypeStruct((n, COL), F32),
        ],
        scratch_shapes=[pltpu.VMEM((tm, D_MODEL), BF16)],
        compiler_params=_cparams(("parallel", "arbitrary")),
        name="in_proj_rows",
    )(x, w_in)


def _in_proj_prompt(x, w_in, w_kv_t, layer, depth, batch, seq, kt_prev, vt_prev):
    n = x.shape[0]
    tm = min(2048, seq)
    nt = seq // tm
    n_alias = 0 if kt_prev is None else 2
    kv_spec = pl.BlockSpec((None, None, COL, tm), lambda i, j: (layer, i // nt, 0, i % nt))
    in_specs = [
        pl.BlockSpec((tm, D_MODEL), lambda i, j: (i, 0)),
        pl.BlockSpec((None, D_MODEL, COL), lambda i, j: (layer, 0, j)),
        pl.BlockSpec((None, COL, D_MODEL), lambda i, j: (layer, jnp.clip(j - 1, 0, 1), 0)),
    ]
    args = [x, w_in, w_kv_t]
    aliases = {}
    if n_alias:
        in_specs += [pl.BlockSpec(memory_space=pl.ANY)] * 2
        args += [kt_prev, vt_prev]
        aliases = {3: 1, 4: 2}
    kv_shape = jax.ShapeDtypeStruct((depth, batch, COL, seq), F32)
    return pl.pallas_call(
        functools.partial(_in_proj_kernel, feature_major=True, n_alias=n_alias),
        grid=(n // tm, N_COLBLK),
        in_specs=in_specs,
        out_specs=[
            pl.BlockSpec((None, tm, COL), lambda i, j: (_slot_of_colblk(j), i, 0)),
            kv_spec, kv_spec,
        ],
        out_shape=[jax.ShapeDtypeStruct((N_SLOTS, n, COL), F32), kv_shape, kv_shape],
        scratch_shapes=[pltpu.VMEM((tm, D_MODEL), BF16)],
        input_output_aliases=aliases,
        compiler_params=_cparams(("parallel", "arbitrary")),
        name="in_proj_prompt",
    )(*args)


def _sb_suffix(zs, suffix_m, causals):
    blk = suffix_m.shape[1]
    nblk = zs[0].shape[1] // blk
    nls = [_softplus2(z) for z in zs]
    nls = [nl if m is None else jnp.where(m, nl, 0.0) for nl, m in zip(nls, causals)]

    def stacked_split(nl):
        halves = _split2(nl) if suffix_m.shape[0] == 2 * blk else (nl.astype(BF16),)
        return jnp.concatenate(
            [jnp.concatenate([h[:, j * blk:(j + 1) * blk] for h in halves], axis=1)
             for j in range(nblk)], axis=0)

    return nls, [_dot(stacked_split(nl), suffix_m) for nl in nls]


def _sb_finish(zs, nls, suffix, cs, causals, chained=False):
    blk = suffix[0].shape[1]
    nblk = zs[0].shape[1] // blk
    rows = zs[0].shape[0]
    out_a, out_c = [], []
    c = None
    for z, nl, suf, c_in, m in zip(zs, nls, suffix, cs, causals):
        c = c_in if (c is None or not chained) else c
        parts = [None] * nblk
        for j in reversed(range(nblk)):
            sl = slice(j * blk, (j + 1) * blk)
            parts[j] = jnp.exp2(z[:, sl] - suf[j * rows:(j + 1) * rows, :] - c)
            c = c + jnp.sum(nl[:, sl], axis=-1, keepdims=True)
        a = parts[0] if nblk == 1 else jnp.concatenate(parts, axis=1)
        out_a.append(a if m is None else jnp.where(m, a, 0.0))
        out_c.append(c)
    return out_a, out_c


def _attn_prompt_kernel(bias_ref, q_ref, kt_ref, vt_ref, o_ref, *, tile, row_chunk, key_chunks):
    pair = pl.program_id(1)
    qi = pl.program_id(2)
    q = q_ref[...] * (LOG2E / math.sqrt(HEAD_DIM_A))
    lane = lax.broadcasted_iota(jnp.int32, (tile, LANES), 1)
    q0 = jnp.where(lane < HEAD_DIM_A, q, 0.0)
    q_heads = (q0.astype(BF16), (q - q0).astype(BF16))
    biases = (bias_ref[2 * pair] * LOG2E, bias_ref[2 * pair + 1] * LOG2E)
    rc = min(row_chunk, tile)
    nrc = tile // rc
    kfac = math.gcd(key_chunks, nrc)
    kb = kfac * rc
    chains = [(hh, r) for hh in range(2) for r in range(nrc)]
    q_chain = [q_heads[hh][r * rc:(r + 1) * rc] for hh, r in chains]
    row = lax.broadcasted_iota(jnp.int32, (rc, kb), 0)
    col = lax.broadcasted_iota(jnp.int32, (rc, kb), 1)
    suffix_m = _suffix_matrix(MXU_DIM if rc % MXU_DIM == 0 else rc, 1)
    feat = lax.broadcasted_iota(jnp.int32, (LANES, kb), 0)
    head0_feat = feat < HEAD_DIM_A

    def step(kj, cs, ots, diag):
        start = pl.multiple_of(kj * kb, kb)
        kt = kt_ref[:, pl.ds(start, kb)].astype(BF16)
        vt = vt_ref[:, pl.ds(start, kb)]
        vt_heads = (jnp.where(head0_feat, vt, 0.0).astype(BF16),
                    jnp.where(head0_feat, 0.0, vt).astype(BF16))
        first = 0 if diag is None else diag * kfac
        live = [i for i, (_, r) in enumerate(chains) if r >= first]
        masks = [col < row + (chains[i][1] - first) * rc
                 if diag is not None and chains[i][1] < first + kfac else None for i in live]
        zs = [_dot(q_chain[i], kt) + biases[chains[i][0]] for i in live]
        nls, suffix = _sb_suffix(zs, suffix_m, masks)
        a_s, c_new = _sb_finish(zs, nls, suffix, [cs[i] for i in live], masks)
        cs, ots = list(cs), list(ots)
        for i, a, c in zip(live, a_s, c_new):
            hh, r = chains[i]
            cs[i] = c
            ots[r] = ots[r] + _dot_nt(vt_heads[hh], a.astype(BF16))
        return tuple(cs), tuple(ots)

    cs = tuple(jnp.zeros((rc, 1), F32) for _ in chains)
    ots = tuple(jnp.zeros((LANES, rc), F32) for _ in range(nrc))
    n_diag = nrc // kfac
    for d in reversed(range(n_diag)):
        cs, ots = step(qi * n_diag + d, cs, ots, d)
    def below(jj, co):
        for u in range(n_diag):
            co = step((qi - jj) * n_diag - 1 - u, co[0], co[1], None)
        return co

    cs, ots = lax.fori_loop(0, qi, below, (cs, ots))
    for r in range(nrc):
        o_ref[r * rc:(r + 1) * rc, :] = ots[r].T.astype(o_ref.dtype)


def _attn_prompt(slots, kt_all, vt_all, sb_bias, layer, batch, seq):
    tile = min(ATTN_Q_TILE, seq)
    nq = seq // tile
    n = batch * seq
    kv_spec = pl.BlockSpec((None, None, LANES, seq), lambda b, p, i: (layer, b, p, 0))
    return pl.pallas_call(
        functools.partial(_attn_prompt_kernel, tile=tile, row_chunk=ATTN_ROW_CHUNK,
                          key_chunks=ATTN_KEY_CHUNKS),
        grid=(batch, WIDTH_A // LANES, nq),
        in_specs=[
            pl.BlockSpec(memory_space=pltpu.SMEM),
            pl.BlockSpec((None, tile, LANES), lambda b, p, i: (SL_QA, b * nq + i, p)),
            kv_spec, kv_spec,
        ],
        out_specs=pl.BlockSpec((tile, LANES), lambda b, p, i: (b * nq + i, p)),
        out_shape=jax.ShapeDtypeStruct((n, WIDTH_A), _branch_dtype(tile)),
        compiler_params=_cparams(("parallel", "parallel", "arbitrary")),
        name="sb_attn_prompt",
    )(sb_bias, slots, kt_all, vt_all)


def _attn_decode_kernel(pt_ref, bias_ref, q_ref, kn_ref, vn_ref, *rest, n_new, page, pps):
    k_refs = rest[:pps]
    v_refs = rest[pps:2 * pps]
    o_ref = rest[2 * pps]
    qbd_scr, c_scr, acc_scr, kpad_scr, vpad_scr = rest[2 * pps + 1:]
    g = pl.program_id(1)
    rows = N_HEADS_A * n_new
    bias = bias_ref[...] * LOG2E

    @pl.when(g == 0)
    def _():
        q = q_ref[...] * (LOG2E / math.sqrt(HEAD_DIM_A))
        qt = jnp.concatenate([q] * N_HEADS_A, axis=0)
        r = lax.broadcasted_iota(jnp.int32, (rows, WIDTH_A), 0)
        f = lax.broadcasted_iota(jnp.int32, (rows, WIDTH_A), 1)
        qbd = jnp.where(r // n_new == f // HEAD_DIM_A, qt, 0.0).astype(BF16)
        qbd_scr[...] = qbd
        kpad_scr[...] = jnp.zeros_like(kpad_scr)
        vpad_scr[...] = jnp.zeros_like(vpad_scr)
        kpad_scr[0:n_new, :] = kn_ref[...]
        vpad_scr[0:n_new, :] = vn_ref[...]
        rr = lax.broadcasted_iota(jnp.int32, (rows, page), 0)
        ss = lax.broadcasted_iota(jnp.int32, (rows, page), 1)
        causal = ss < (rr % n_new)
        z = _dot_nt(qbd, kpad_scr[...].astype(BF16)) + bias
        nls, suffix = _sb_suffix([z], _suffix_matrix(page, 2), [causal])
        (a,), (c,) = _sb_finish([z], nls, suffix, [jnp.zeros((rows, 1), F32)], [causal])
        acc_scr[...] = _dot(a.astype(BF16), vpad_scr[...].astype(BF16))
        c_scr[...] = c

    kt = jnp.concatenate([r[...] for r in k_refs], axis=1).astype(BF16)
    vt = jnp.concatenate([r[...] for r in v_refs], axis=1).astype(BF16)
    z = _dot(qbd_scr[...], kt) + bias
    blk = math.gcd(MXU_DIM, pps * page)
    nls, suffix = _sb_suffix([z], _suffix_matrix(blk, 2), [None])
    (a,), (c,) = _sb_finish([z], nls, suffix, [c_scr[...]], [None])
    acc_scr[...] += _dot_nt(a.astype(BF16), vt)
    c_scr[...] = c

    @pl.when(g == pl.num_programs(1) - 1)
    def _():
        acc = acc_scr[...]
        f = lax.broadcasted_iota(jnp.int32, (n_new, WIDTH_A), 1)
        out = jnp.zeros((n_new, WIDTH_A), F32)
        for h in range(N_HEADS_A):
            out = out + jnp.where(f // HEAD_DIM_A == h, acc[h * n_new:(h + 1) * n_new, :], 0.0)
        o_ref[...] = out


def _attn_decode(slots, k_new, v_new, cache_kt, cache_vt, page_table, sb_bias, layer, dec_b, n_new):
    page = cache_kt.shape[3]
    n_pages = page_table.shape[1]
    pps = math.gcd(PAGES_PER_STEP, n_pages)
    steps = n_pages // pps
    rows = N_HEADS_A * n_new
    bias_rows = jnp.repeat(sb_bias.astype(F32), n_new)[:, None]

    def page_spec(i):
        return pl.BlockSpec(
            (None, None, WIDTH_A, page),
            lambda b, g, pt: (layer, pt[b, n_pages - (g + 1) * pps + i], 0, 0))

    new_spec = pl.BlockSpec((n_new, COL), lambda b, g, pt: (b, 0))
    grid_spec = pltpu.PrefetchScalarGridSpec(
        num_scalar_prefetch=1,
        grid=(dec_b, steps),
        in_specs=[
            pl.BlockSpec((rows, 1), lambda b, g, pt: (0, 0)),
            pl.BlockSpec((None, n_new, COL), lambda b, g, pt: (SL_QA, b, 0)),
            new_spec, new_spec,
        ] + [page_spec(i) for i in range(pps)] + [page_spec(i) for i in range(pps)],
        out_specs=pl.BlockSpec((n_new, WIDTH_A), lambda b, g, pt: (b, 0)),
        scratch_shapes=[
            pltpu.VMEM((rows, WIDTH_A), BF16),
            pltpu.VMEM((rows, 1), F32),
            pltpu.VMEM((rows, WIDTH_A), F32),
            pltpu.VMEM((page, WIDTH_A), F32),
            pltpu.VMEM((page, WIDTH_A), F32),
        ],
    )
    return pl.pallas_call(
        functools.partial(_attn_decode_kernel, n_new=n_new, page=page, pps=pps),
        grid_spec=grid_spec,
        out_shape=jax.ShapeDtypeStruct((dec_b * n_new, WIDTH_A), F32),
        compiler_params=_cparams(("parallel", "arbitrary")),
        name="sb_attn_decode",
    )(page_table, bias_rows, slots, k_new, v_new, *([cache_kt] * pps), *([cache_vt] * pps))


def _hgrn_kernel(lb_ref, ng_ref, q_ref, f_ref, i_ref, g_ref, s0_ref, o_ref, s_ref,
                 st_scr, b_scr, q_scr, k_scr, o_scr, *, chunk, sub):
    c = pl.program_id(1)

    @pl.when(c == 0)
    def _():
        st_scr[...] = s0_ref[...]

    lb = lb_ref[...]
    zf = f_ref[...]
    a1 = jnp.log(lb)
    a2 = jnp.log1p(-lb) - _softplus(-zf)
    logf = jnp.maximum(a1, a2) + jnp.log(1.0 + jnp.exp(-jnp.abs(a1 - a2)))
    k_scr[...] = (1.0 - lb) * jax.nn.sigmoid(-zf)
    qv = q_ref[...]
    q_scr[...] = qv * jax.nn.sigmoid(qv)
    r = lax.broadcasted_iota(jnp.int32, (chunk, chunk), 0)
    s = lax.broadcasted_iota(jnp.int32, (chunk, chunk), 1)
    lmat = jnp.where((s <= r) & (s // sub == r // sub), 1.0, 0.0).astype(BF16)
    h1, h2, h3 = _split3(logf)
    b_scr[...] = (_dot(lmat, h1) + _dot(lmat, h2) + _dot(lmat, h3)) * LOG2E

    grp = math.gcd(sub, SUBLANES)
    ngrp = sub // grp
    gidx = lax.broadcasted_iota(jnp.int32, (grp, 1), 0)

    states = [st_scr[h] for h in range(N_HEADS_B)]
    for ci in range(chunk // sub):
        rows = slice(ci * sub, (ci + 1) * sub)
        heads = []
        for h in range(N_HEADS_B):
            cols = slice(h * DK_B, (h + 1) * DK_B)
            b = b_scr[rows, cols]
            qs = q_scr[rows, cols]
            kk = k_scr[rows, cols]
            vv = i_ref[rows, cols]
            b_last = b[sub - 1:sub, :]
            st = states[h]
            o_state = _dot_nt((qs * jnp.exp2(b)).astype(BF16), st.astype(BF16))
            ke = kk * jnp.exp2(b_last - b)
            states[h] = st * jnp.exp2(b_last) + _dot_tn(vv.astype(BF16), ke.astype(BF16))
            heads.append((cols, b, qs, kk, vv, o_state))
        for cols, b, qs, kk, vv, o_state in heads:
            og = [None] * ngrp
            bg = [b[g * grp:(g + 1) * grp] for g in range(ngrp)]
            qg = [qs[g * grp:(g + 1) * grp] for g in range(ngrp)]
            for t in range(sub):
                bt = b[t:t + 1, :]
                kt = kk[t:t + 1, :]
                vt = vv[t:t + 1, :]
                for g in range(t // grp, ngrp):
                    if g == t // grp:
                        w = jnp.exp2(jnp.minimum(bg[g] - bt, 0.0))
                        p = jnp.sum(qg[g] * kt * w, axis=-1, keepdims=True)
                        p = jnp.where(gidx >= t - g * grp, p, 0.0)
                    else:
                        p = jnp.sum(qg[g] * kt * jnp.exp2(bg[g] - bt), axis=-1, keepdims=True)
                    og[g] = p * vt if og[g] is None else og[g] + p * vt
            o_scr[rows, cols] = (og[0] if ngrp == 1 else jnp.concatenate(og, axis=0)) + o_state
    for h in range(N_HEADS_B):
        st_scr[h] = states[h]

    gate = g_ref[...]
    gate = gate * jax.nn.sigmoid(gate)
    ng = ng_ref[...]
    for h in range(N_HEADS_B):
        cols = slice(h * DV_B, (h + 1) * DV_B)
        o = o_scr[:, cols]
        o = o * lax.rsqrt(jnp.mean(o * o, axis=-1, keepdims=True) + RMS_EPS) * ng
        o_ref[:, cols] = (o * gate[:, cols]).astype(o_ref.dtype)

    @pl.when(c == pl.num_programs(1) - 1)
    def _():
        s_ref[...] = st_scr[...]


def _hgrn(slots, lb_row, norm_g_row, s0_t, batch, seq):
    chunk = min(128, seq)
    sub = min(SUBCHUNK_B, chunk)
    nc = seq // chunk
    n = batch * seq
    blk = lambda sl: pl.BlockSpec((None, chunk, COL), lambda b, c: (sl, b * nc + c, 0))
    st_spec = pl.BlockSpec((None, N_HEADS_B, DV_B, DK_B), lambda b, c: (b, 0, 0, 0))
    return pl.pallas_call(
        functools.partial(_hgrn_kernel, chunk=chunk, sub=sub),
        grid=(batch, nc),
        in_specs=[
            pl.BlockSpec((1, WIDTH_B), lambda b, c: (0, 0)),
            pl.BlockSpec((1, DV_B), lambda b, c: (0, 0)),
            blk(SL_QB), blk(SL_FB), blk(SL_IB), blk(SL_GB),
            st_spec,
        ],
        out_specs=[
            pl.BlockSpec((chunk, COL), lambda b, c: (b * nc + c, 0)),
            st_spec,
        ],
        out_shape=[
            jax.ShapeDtypeStruct((n, COL), _branch_dtype(chunk)),
            jax.ShapeDtypeStruct((batch, N_HEADS_B, DV_B, DK_B), F32),
        ],
        scratch_shapes=[
            pltpu.VMEM((N_HEADS_B, DV_B, DK_B), F32),
            pltpu.VMEM((chunk, WIDTH_B), F32),
            pltpu.VMEM((chunk, WIDTH_B), F32),
            pltpu.VMEM((chunk, WIDTH_B), F32),
            pltpu.VMEM((chunk, COL), F32),
        ],
        compiler_params=_cparams(("parallel", "arbitrary")),
        name="hgrn2",
    )(lb_row, norm_g_row, slots, slots, slots, slots, s0_t)


def _merge_ln_kernel(x_ref, oa_ref, ob_ref, za0_ref, za1_ref, zb0_ref, zb1_ref, bm_ref,
                     woa_ref, wob_ref, wout_ref, g_ref, b_ref, o_ref, *, alpha):
    ya = _dot(oa_ref[...].astype(BF16), woa_ref[...])
    yb = _dot(ob_ref[...].astype(BF16), wob_ref[...])
    za = jnp.concatenate([za0_ref[...], za1_ref[...]], axis=-1)
    zb = jnp.concatenate([zb0_ref[...], zb1_ref[...]], axis=-1)
    bm = bm_ref[...]
    m = jax.nn.sigmoid(za + bm[0:1, :]) * ya + jax.nn.sigmoid(zb + bm[1:2, :]) * yb
    mix = _dot(m.astype(BF16), wout_ref[...])
    o_ref[...] = _layer_norm(alpha * x_ref[...] + mix, g_ref[...], b_ref[...])


def _merge_ln(x, oa, ob, slots, b_merge, w_oa, w_ob, w_out, ln_g, ln_b, layer, alpha):
    n = x.shape[0]
    tm = min(512, n)
    row = lambda w: pl.BlockSpec((tm, w), lambda i: (i, 0))
    zblk = lambda sl: pl.BlockSpec((None, tm, COL), lambda i: (sl, i, 0))
    wspec = lambda k: pl.BlockSpec((None, k, D_MODEL), lambda i: (layer, 0, 0))
    ln_spec = pl.BlockSpec((None, None, 1, D_MODEL), lambda i: (layer, 1, 0, 0))
    return pl.pallas_call(
        functools.partial(_merge_ln_kernel, alpha=alpha),
        grid=(n // tm,),
        in_specs=[
            row(D_MODEL), row(WIDTH_A), row(COL),
            zblk(SL_ZA), zblk(SL_ZA + 1), zblk(SL_ZB), zblk(SL_ZB + 1),
            pl.BlockSpec((None, 2, D_MODEL), lambda i: (layer, 0, 0)),
            wspec(WIDTH_A), wspec(COL), wspec(D_MODEL),
            ln_spec, ln_spec,
        ],
        out_specs=row(D_MODEL),
        out_shape=jax.ShapeDtypeStruct((n, D_MODEL), F32),
        compiler_params=_cparams(("parallel",)),
        name="merge_ln",
    )(x, oa, ob, slots, slots, slots, slots, b_merge, w_oa, w_ob, w_out, ln_g, ln_b)


def _embed_ln_kernel(x_ref, p_ref, wg_ref, wp_ref, bg_ref, g_ref, b_ref, o_ref, *, alpha):
    x = x_ref[...]
    gate = jax.nn.sigmoid(_dot(x.astype(BF16), wg_ref[...]) + bg_ref[...])
    e = gate * _dot(p_ref[...].astype(BF16), wp_ref[...])
    o_ref[...] = _layer_norm(alpha * x + e, g_ref[...], b_ref[...])


def _embed_ln(x, p, pe_w_gate, pe_w_proj, pe_b_gate, ln_g, ln_b, layer, alpha):
    n = x.shape[0]
    tm = min(512, n)
    ln_spec = pl.BlockSpec((None, None, 1, D_MODEL), lambda i: (layer, 3, 0, 0))
    return pl.pallas_call(
        functools.partial(_embed_ln_kernel, alpha=alpha),
        grid=(n // tm,),
        in_specs=[
            pl.BlockSpec((tm, D_MODEL), lambda i: (i, 0)),
            pl.BlockSpec((None, tm, P_DIM), lambda i: (layer, i, 0)),
            pl.BlockSpec((None, D_MODEL, D_MODEL), lambda i: (layer, 0, 0)),
            pl.BlockSpec((None, P_DIM, D_MODEL), lambda i: (layer, 0, 0)),
            pl.BlockSpec((None, 1, D_MODEL), lambda i: (layer, 0, 0)),
            ln_spec, ln_spec,
        ],
        out_specs=pl.BlockSpec((tm, D_MODEL), lambda i: (i, 0)),
        out_shape=jax.ShapeDtypeStruct((n, D_MODEL), F32),
        compiler_params=_cparams(("parallel",)),
        name="embed_ln",
    )(x, p, pe_w_gate, pe_w_proj, pe_b_gate, ln_g, ln_b)


def kernel(x_prompt, x_sample, cache_k, cache_v, state_hgrn, page_table, p_prompt, p_sample, ln_g, ln_b, ffn_w_gate, ffn_w_up, ffn_w_down, w_in, b_merge, sb_bias, hgrn_lb, hgrn_norm_g, w_oa, w_ob, w_out, pe_w_proj, pe_w_gate, pe_b_gate):
    depth = ln_g.shape[0]
    alpha = (2.0 * depth) ** 0.25
    batch, seq, _ = x_prompt.shape
    dec_b, dec_seq, _ = x_sample.shape
    n_pool, page = cache_k.shape[1], cache_k.shape[2]

    lb_all = jnp.cumsum(jax.nn.softmax(hgrn_lb.astype(F32), axis=0), axis=0)
    lb_all = lb_all - lb_all[0:1]

    wg = ffn_w_gate.astype(BF16)
    wu = ffn_w_up.astype(BF16)
    wd = ffn_w_down.astype(BF16)
    w_in_b = w_in.astype(BF16)
    w_kv_t = jnp.swapaxes(w_in_b[:, :, COL:3 * COL], 1, 2)
    w_oa_b = w_oa.astype(BF16)
    w_ob_b = w_ob.astype(BF16)
    w_out_b = w_out.astype(BF16)
    pe_wg_b = pe_w_gate.astype(BF16)
    pe_wp_b = pe_w_proj.astype(BF16)
    ln_g4 = ln_g.reshape(depth, 4, 1, D_MODEL)
    ln_b4 = ln_b.reshape(depth, 4, 1, D_MODEL)
    pe_bg3 = pe_b_gate.reshape(depth, 1, D_MODEL)
    cache_kt = jnp.transpose(cache_k, (0, 1, 3, 4, 2)).reshape(depth, n_pool, WIDTH_A, page)
    cache_vt = jnp.transpose(cache_v, (0, 1, 3, 4, 2)).reshape(depth, n_pool, WIDTH_A, page)
    p_prompt3 = p_prompt.reshape(depth, batch * seq, P_DIM)
    p_sample3 = p_sample.reshape(depth, dec_b * dec_seq, P_DIM)
    state_t = jnp.swapaxes(state_hgrn, -1, -2)
    zero_state = jnp.zeros((batch, N_HEADS_B, DV_B, DK_B), F32)

    def layer_tail(x, p3, i, slots, oa, ob):
        x = _merge_ln(x, oa, ob, slots, b_merge, w_oa_b, w_ob_b, w_out_b, ln_g4, ln_b4, i, alpha)
        x = _ffn_ln(x, wg, wu, wd, ln_g4, ln_b4, i, 1, 2, alpha)
        return _embed_ln(x, p3, pe_wg_b, pe_wp_b, pe_bg3, ln_g4, ln_b4, i, alpha)

    yp = x_prompt.reshape(batch * seq, D_MODEL)
    ys = x_sample.reshape(dec_b * dec_seq, D_MODEL)
    kt_all = vt_all = None
    sp, kd, vd, sd = [], [], [], []
    for i in range(depth):
        lb_row = lb_all[i][None, :]
        ng_row = hgrn_norm_g[i][None, :]
        yp = _ffn_ln(yp, wg, wu, wd, ln_g4, ln_b4, i, 0, 0, alpha)
        slots, kt_all, vt_all = _in_proj_prompt(yp, w_in_b, w_kv_t, i, depth, batch, seq, kt_all, vt_all)
        oa = _attn_prompt(slots, kt_all, vt_all, sb_bias[i], i, batch, seq)
        ob, s_t = _hgrn(slots, lb_row, ng_row, zero_state, batch, seq)
        yp = layer_tail(yp, p_prompt3, i, slots, oa, ob)
        sp.append(jnp.swapaxes(s_t, -1, -2))
        ys = _ffn_ln(ys, wg, wu, wd, ln_g4, ln_b4, i, 0, 0, alpha)
        slots, k_new, v_new = _in_proj_rows(ys, w_in_b, i)
        oa = _attn_decode(slots, k_new, v_new, cache_kt, cache_vt, page_table, sb_bias[i], i,
                          dec_b, dec_seq)
        ob, s_t = _hgrn(slots, lb_row, ng_row, state_t[i], dec_b, dec_seq)
        ys = layer_tail(ys, p_sample3, i, slots, oa, ob)
        kd.append(k_new.reshape(dec_b, dec_seq, N_HEADS_A, HEAD_DIM_A))
        vd.append(v_new.reshape(dec_b, dec_seq, N_HEADS_A, HEAD_DIM_A))
        sd.append(jnp.swapaxes(s_t, -1, -2))

    def token_major(t_all):
        t5 = t_all.reshape(depth, batch, N_HEADS_A, HEAD_DIM_A, seq)
        return jnp.transpose(t5, (0, 1, 4, 2, 3))

    return (yp.reshape(batch, seq, D_MODEL), ys.reshape(dec_b, dec_seq, D_MODEL),
            token_major(kt_all), token_major(vt_all), jnp.stack(sp),
            jnp.stack(kd), jnp.stack(vd), jnp.stack(sd))
```

```python
import functools
import math

import jax
import jax.numpy as jnp
from jax import lax
from jax.experimental import pallas as pl
from jax.experimental.pallas import tpu as pltpu

F32 = jnp.float32
BF16 = jnp.bfloat16

D_MODEL = 1024
N_HEADS_A = 8
HEAD_DIM_A = 64
WIDTH_A = N_HEADS_A * HEAD_DIM_A
N_HEADS_B = 4
DK_B = 128
DV_B = 128
WIDTH_B = N_HEADS_B * DK_B
D_FF = 2816
P_DIM = 256
LN_EPS = 1e-5
RMS_EPS = 1e-6
LOG2E = 1.4426950408889634
COL = 512
N_COLBLK = 11
N_SLOTS = N_COLBLK - 2
(SL_QA, SL_QB, SL_FB, SL_IB, SL_GB, SL_ZA, SL_ZB) = (0, 1, 2, 3, 4, 5, 7)

LANES = 128
SUBLANES = 8
BF16_SUBLANES = 16
MXU_DIM = 256
VMEM_LIMIT_BYTES = 56 * 1024 * 1024
FFN_TILE_F = D_FF // 2
FFN_TILE_ROWS = 1024
IN_PROJ_TILE_ROWS = 512
SUBCHUNK_B = 16
PAGES_PER_STEP = 32
ATTN_Q_TILE = 1024
ATTN_ROW_CHUNK = 256
ATTN_KEY_CHUNKS = 2


def _cparams(sem):
    return pltpu.CompilerParams(dimension_semantics=sem, vmem_limit_bytes=VMEM_LIMIT_BYTES)


def _branch_dtype(block_rows):
    return BF16 if block_rows % BF16_SUBLANES == 0 else F32


def _dot(a, b):
    return jnp.dot(a, b, preferred_element_type=F32)


def _dot_nt(a, b):
    return lax.dot_general(a, b, (((1,), (1,)), ((), ())), preferred_element_type=F32)


def _dot_tn(a, b):
    return lax.dot_general(a, b, (((0,), (0,)), ((), ())), preferred_element_type=F32)


def _split2(x):
    hi = x.astype(BF16)
    lo = (x - hi.astype(F32)).astype(BF16)
    return hi, lo


def _split3(x):
    h1 = x.astype(BF16)
    r = x - h1.astype(F32)
    h2 = r.astype(BF16)
    h3 = (r - h2.astype(F32)).astype(BF16)
    return h1, h2, h3


def _softplus(z):
    return jnp.maximum(z, 0.0) + jnp.log(1.0 + jnp.exp(-jnp.abs(z)))


def _softplus2(z2):
    return jnp.maximum(z2, 0.0) + jnp.log(1.0 + jnp.exp2(-jnp.abs(z2))) * LOG2E


def _layer_norm(y, g, b):
    mu = jnp.mean(y, axis=-1, keepdims=True)
    yc = y - mu
    var = jnp.mean(yc * yc, axis=-1, keepdims=True)
    return yc * lax.rsqrt(var + LN_EPS) * g + b


def _suffix_matrix(n, halves):
    row = lax.broadcasted_iota(jnp.int32, (halves * n, n), 0)
    col = lax.broadcasted_iota(jnp.int32, (halves * n, n), 1)
    return jnp.where(jnp.where(row >= n, row - n, row) >= col, 1.0, 0.0).astype(BF16)


def _ffn_ln_kernel(x_ref, wg_ref, wu_ref, wd_ref, g_ref, b_ref, o_ref, *, alpha, sub_rows, tile_f):
    for s in range(x_ref.shape[0] // sub_rows):
        rows = slice(s * sub_rows, (s + 1) * sub_rows)
        x = x_ref[rows, :]
        xb = x.astype(BF16)
        acc = None
        for f in range(D_FF // tile_f):
            cols = slice(f * tile_f, (f + 1) * tile_f)
            g = _dot(xb, wg_ref[:, cols])
            u = _dot(xb, wu_ref[:, cols])
            h = (g * jax.nn.sigmoid(g) * u).astype(BF16)
            part = _dot(h, wd_ref[cols, :])
            acc = part if acc is None else acc + part
        o_ref[rows, :] = _layer_norm(alpha * x + 0.5 * acc, g_ref[...], b_ref[...])


def _ffn_ln(x, wg, wu, wd, ln_g, ln_b, layer, which, ln_idx, alpha):
    n = x.shape[0]
    tm = min(FFN_TILE_ROWS, n)
    sub_rows = tm // 2 if tm % (2 * BF16_SUBLANES) == 0 else tm
    resident = pl.Buffered(1)
    return pl.pallas_call(
        functools.partial(_ffn_ln_kernel, alpha=alpha, sub_rows=sub_rows, tile_f=FFN_TILE_F),
        grid=(n // tm,),
        in_specs=[
            pl.BlockSpec((tm, D_MODEL), lambda i: (i, 0)),
            pl.BlockSpec((None, None, D_MODEL, D_FF), lambda i: (layer, which, 0, 0), resident),
            pl.BlockSpec((None, None, D_MODEL, D_FF), lambda i: (layer, which, 0, 0), resident),
            pl.BlockSpec((None, None, D_FF, D_MODEL), lambda i: (layer, which, 0, 0), resident),
            pl.BlockSpec((None, None, 1, D_MODEL), lambda i: (layer, ln_idx, 0, 0)),
            pl.BlockSpec((None, None, 1, D_MODEL), lambda i: (layer, ln_idx, 0, 0)),
        ],
        out_specs=pl.BlockSpec((tm, D_MODEL), lambda i: (i, 0)),
        out_shape=jax.ShapeDtypeStruct((n, D_MODEL), F32),
        compiler_params=_cparams(("parallel",)),
        name="ffn_ln",
    )(x, wg, wu, wd, ln_g, ln_b)


def _in_proj_kernel(x_ref, w_ref, *rest, feature_major, n_alias):
    if feature_major:
        wkv_ref = rest[0]
        rest = rest[1:]
    o_ref, k_ref, v_ref = rest[n_alias:]
    xb = x_ref[...].astype(BF16)
    for j in range(N_COLBLK):
        if j in (1, 2):
            ref = k_ref if j == 1 else v_ref
            if feature_major:
                ref[...] = _dot_nt(wkv_ref[(j - 1) * COL:j * COL, :], xb)
            else:
                ref[...] = _dot(xb, w_ref[:, j * COL:(j + 1) * COL])
        else:
            o_ref[0 if j == 0 else j - 2] = _dot(xb, w_ref[:, j * COL:(j + 1) * COL])


def _in_proj_rows(x, w_in, layer):
    n = x.shape[0]
    tm = min(IN_PROJ_TILE_ROWS, n)
    kv_spec = pl.BlockSpec((tm, COL), lambda i: (i, 0))
    return pl.pallas_call(
        functools.partial(_in_proj_kernel, feature_major=False, n_alias=0),
        grid=(n // tm,),
        in_specs=[
            pl.BlockSpec((tm, D_MODEL), lambda i: (i, 0)),
            pl.BlockSpec((None, D_MODEL, N_COLBLK * COL), lambda i: (layer, 0, 0), pl.Buffered(1)),
        ],
        out_specs=[
            pl.BlockSpec((N_SLOTS, tm, COL), lambda i: (0, i, 0)),
            kv_spec, kv_spec,
        ],
        out_shape=[
            jax.ShapeDtypeStruct((N_SLOTS, n, COL), F32),
            jax.ShapeDtypeStruct((n, COL), F32),
            jax.ShapeDtypeStruct((n, COL), F32),
        ],
        compiler_params=_cparams(("parallel",)),
        name="in_proj_rows",
    )(x, w_in)


def _in_proj_prompt(x, w_in, w_kv_t, layer, depth, batch, seq, kt_prev, vt_prev):
    n = x.shape[0]
    tm = min(IN_PROJ_TILE_ROWS, seq)
    nt = seq // tm
    n_alias = 0 if kt_prev is None else 2
    kv_spec = pl.BlockSpec((None, None, COL, tm), lambda i: (layer, i // nt, 0, i % nt))
    resident = pl.Buffered(1)
    in_specs = [
        pl.BlockSpec((tm, D_MODEL), lambda i: (i, 0)),
        pl.BlockSpec((None, D_MODEL, N_COLBLK * COL), lambda i: (layer, 0, 0), resident),
        pl.BlockSpec((None, 2 * COL, D_MODEL), lambda i: (layer, 0, 0), resident),
    ]
    args = [x, w_in, w_kv_t]
    aliases = {}
    if n_alias:
        in_specs += [pl.BlockSpec(memory_space=pl.ANY)] * 2
        args += [kt_prev, vt_prev]
        aliases = {3: 1, 4: 2}
    kv_shape = jax.ShapeDtypeStruct((depth, batch, COL, seq), F32)
    return pl.pallas_call(
        functools.partial(_in_proj_kernel, feature_major=True, n_alias=n_alias),
        grid=(n // tm,),
        in_specs=in_specs,
        out_specs=[
            pl.BlockSpec((N_SLOTS, tm, COL), lambda i: (0, i, 0)),
            kv_spec, kv_spec,
        ],
        out_shape=[jax.ShapeDtypeStruct((N_SLOTS, n, COL), F32), kv_shape, kv_shape],
        input_output_aliases=aliases,
        compiler_params=_cparams(("parallel",)),
        name="in_proj_prompt",
    )(*args)


def _sb_suffix(zs, suffix_m, causals):
    blk = suffix_m.shape[1]
    nblk = zs[0].shape[1] // blk
    nls = [_softplus2(z) for z in zs]
    nls = [nl if m is None else jnp.where(m, nl, 0.0) for nl, m in zip(nls, causals)]

    def stacked_split(nl):
        halves = _split2(nl) if suffix_m.shape[0] == 2 * blk else (nl.astype(BF16),)
        return jnp.concatenate(
            [jnp.concatenate([h[:, j * blk:(j + 1) * blk] for h in halves], axis=1)
             for j in range(nblk)], axis=0)

    return nls, [_dot(stacked_split(nl), suffix_m) for nl in nls]


def _sb_finish(zs, nls, suffix, cs, causals, chained=False):
    blk = suffix[0].shape[1]
    nblk = zs[0].shape[1] // blk
    rows = zs[0].shape[0]
    out_a, out_c = [], []
    c = None
    for z, nl, suf, c_in, m in zip(zs, nls, suffix, cs, causals):
        c = c_in if (c is None or not chained) else c
        parts = [None] * nblk
        for j in reversed(range(nblk)):
            sl = slice(j * blk, (j + 1) * blk)
            parts[j] = jnp.exp2(z[:, sl] - suf[j * rows:(j + 1) * rows, :] - c)
            c = c + jnp.sum(nl[:, sl], axis=-1, keepdims=True)
        a = parts[0] if nblk == 1 else jnp.concatenate(parts, axis=1)
        out_a.append(a if m is None else jnp.where(m, a, 0.0))
        out_c.append(c)
    return out_a, out_c


def _attn_prompt_kernel(bias_ref, q_ref, kt_ref, vt_ref, o_ref, *, tile, row_chunk, key_chunks):
    pair = pl.program_id(1)
    qi = pl.program_id(2)
    q = q_ref[...] * (LOG2E / math.sqrt(HEAD_DIM_A))
    lane = lax.broadcasted_iota(jnp.int32, (tile, LANES), 1)
    q0 = jnp.where(lane < HEAD_DIM_A, q, 0.0)
    q_heads = (q0.astype(BF16), (q - q0).astype(BF16))
    biases = (bias_ref[2 * pair] * LOG2E, bias_ref[2 * pair + 1] * LOG2E)
    rc = min(row_chunk, tile)
    nrc = tile // rc
    kfac = math.gcd(key_chunks, nrc)
    kb = kfac * rc
    chains = [(hh, r) for hh in range(2) for r in range(nrc)]
    q_chain = [q_heads[hh][r * rc:(r + 1) * rc] for hh, r in chains]
    row = lax.broadcasted_iota(jnp.int32, (rc, kb), 0)
    col = lax.broadcasted_iota(jnp.int32, (rc, kb), 1)
    suffix_m = _suffix_matrix(MXU_DIM if rc % MXU_DIM == 0 else rc, 1)
    feat = lax.broadcasted_iota(jnp.int32, (LANES, kb), 0)
    head0_feat = feat < HEAD_DIM_A

    def step(kj, cs, ots, diag):
        start = pl.multiple_of(kj * kb, kb)
        kt = kt_ref[:, pl.ds(start, kb)].astype(BF16)
        vt = vt_ref[:, pl.ds(start, kb)]
        vt_heads = (jnp.where(head0_feat, vt, 0.0).astype(BF16),
                    jnp.where(head0_feat, 0.0, vt).astype(BF16))
        first = 0 if diag is None else diag * kfac
        live = [i for i, (_, r) in enumerate(chains) if r >= first]
        masks = [col < row + (chains[i][1] - first) * rc
                 if diag is not None and chains[i][1] < first + kfac else None for i in live]
        zs = [_dot(q_chain[i], kt) + biases[chains[i][0]] for i in live]
        nls, suffix = _sb_suffix(zs, suffix_m, masks)
        a_s, c_new = _sb_finish(zs, nls, suffix, [cs[i] for i in live], masks)
        cs, ots = list(cs), list(ots)
        for i, a, c in zip(live, a_s, c_new):
            hh, r = chains[i]
            cs[i] = c
            ots[r] = ots[r] + _dot_nt(vt_heads[hh], a.astype(BF16))
        return tuple(cs), tuple(ots)

    cs = tuple(jnp.zeros((rc, 1), F32) for _ in chains)
    ots = tuple(jnp.zeros((LANES, rc), F32) for _ in range(nrc))
    n_diag = nrc // kfac
    for d in reversed(range(n_diag)):
        cs, ots = step(qi * n_diag + d, cs, ots, d)
    def below(jj, co):
        for u in range(n_diag):
            co = step((qi - jj) * n_diag - 1 - u, co[0], co[1], None)
        return co

    cs, ots = lax.fori_loop(0, qi, below, (cs, ots))
    for r in range(nrc):
        o_ref[r * rc:(r + 1) * rc, :] = ots[r].T.astype(o_ref.dtype)


def _attn_prompt(slots, kt_all, vt_all, sb_bias, layer, batch, seq):
    tile = min(ATTN_Q_TILE, seq)
    nq = seq // tile
    n = batch * seq
    kv_spec = pl.BlockSpec((None, None, LANES, seq), lambda b, p, i: (layer, b, p, 0))
    return pl.pallas_call(
        functools.partial(_attn_prompt_kernel, tile=tile, row_chunk=ATTN_ROW_CHUNK,
                          key_chunks=ATTN_KEY_CHUNKS),
        grid=(batch, WIDTH_A // LANES, nq),
        in_specs=[
            pl.BlockSpec(memory_space=pltpu.SMEM),
            pl.BlockSpec((None, tile, LANES), lambda b, p, i: (SL_QA, b * nq + i, p)),
            kv_spec, kv_spec,
        ],
        out_specs=pl.BlockSpec((tile, LANES), lambda b, p, i: (b * nq + i, p)),
        out_shape=jax.ShapeDtypeStruct((n, WIDTH_A), _branch_dtype(tile)),
        compiler_params=_cparams(("parallel", "parallel", "arbitrary")),
        name="sb_attn_prompt",
    )(sb_bias, slots, kt_all, vt_all)


def _attn_decode_kernel(pt_ref, bias_ref, q_ref, kn_ref, vn_ref, *rest, n_new, page, pps):
    k_refs = rest[:pps]
    v_refs = rest[pps:2 * pps]
    o_ref = rest[2 * pps]
    qbd_scr, c_scr, acc_scr, kpad_scr, vpad_scr = rest[2 * pps + 1:]
    g = pl.program_id(1)
    rows = N_HEADS_A * n_new
    bias = bias_ref[...] * LOG2E

    @pl.when(g == 0)
    def _():
        q = q_ref[...] * (LOG2E / math.sqrt(HEAD_DIM_A))
        qt = jnp.concatenate([q] * N_HEADS_A, axis=0)
        r = lax.broadcasted_iota(jnp.int32, (rows, WIDTH_A), 0)
        f = lax.broadcasted_iota(jnp.int32, (rows, WIDTH_A), 1)
        qbd = jnp.where(r // n_new == f // HEAD_DIM_A, qt, 0.0).astype(BF16)
        qbd_scr[...] = qbd
        kpad_scr[...] = jnp.zeros_like(kpad_scr)
        vpad_scr[...] = jnp.zeros_like(vpad_scr)
        kpad_scr[0:n_new, :] = kn_ref[...]
        vpad_scr[0:n_new, :] = vn_ref[...]
        rr = lax.broadcasted_iota(jnp.int32, (rows, page), 0)
        ss = lax.broadcasted_iota(jnp.int32, (rows, page), 1)
        causal = ss < (rr % n_new)
        z = _dot_nt(qbd, kpad_scr[...].astype(BF16)) + bias
        nls, suffix = _sb_suffix([z], _suffix_matrix(page, 2), [causal])
        (a,), (c,) = _sb_finish([z], nls, suffix, [jnp.zeros((rows, 1), F32)], [causal])
        acc_scr[...] = _dot(a.astype(BF16), vpad_scr[...].astype(BF16))
        c_scr[...] = c

    kt = jnp.concatenate([r[...] for r in k_refs], axis=1).astype(BF16)
    vt = jnp.concatenate([r[...] for r in v_refs], axis=1).astype(BF16)
    z = _dot(qbd_scr[...], kt) + bias
    blk = math.gcd(MXU_DIM, pps * page)
    nls, suffix = _sb_suffix([z], _suffix_matrix(blk, 2), [None])
    (a,), (c,) = _sb_finish([z], nls, suffix, [c_scr[...]], [None])
    acc_scr[...] += _dot_nt(a.astype(BF16), vt)
    c_scr[...] = c

    @pl.when(g == pl.num_programs(1) - 1)
    def _():
        acc = acc_scr[...]
        f = lax.broadcasted_iota(jnp.int32, (n_new, WIDTH_A), 1)
        out = jnp.zeros((n_new, WIDTH_A), F32)
        for h in range(N_HEADS_A):
            out = out + jnp.where(f // HEAD_DIM_A == h, acc[h * n_new:(h + 1) * n_new, :], 0.0)
        o_ref[...] = out


def _attn_decode(slots, k_new, v_new, cache_kt, cache_vt, page_table, sb_bias, layer, dec_b, n_new):
    page = cache_kt.shape[3]
    n_pages = page_table.shape[1]
    pps = math.gcd(PAGES_PER_STEP, n_pages)
    steps = n_pages // pps
    rows = N_HEADS_A * n_new
    bias_rows = jnp.repeat(sb_bias.astype(F32), n_new)[:, None]

    def page_spec(i):
        return pl.BlockSpec(
            (None, None, WIDTH_A, page),
            lambda b, g, pt: (layer, pt[b, n_pages - (g + 1) * pps + i], 0, 0))

    new_spec = pl.BlockSpec((n_new, COL), lambda b, g, pt: (b, 0))
    grid_spec = pltpu.PrefetchScalarGridSpec(
        num_scalar_prefetch=1,
        grid=(dec_b, steps),
        in_specs=[
            pl.BlockSpec((rows, 1), lambda b, g, pt: (0, 0)),
            pl.BlockSpec((None, n_new, COL), lambda b, g, pt: (SL_QA, b, 0)),
            new_spec, new_spec,
        ] + [page_spec(i) for i in range(pps)] + [page_spec(i) for i in range(pps)],
        out_specs=pl.BlockSpec((n_new, WIDTH_A), lambda b, g, pt: (b, 0)),
        scratch_shapes=[
            pltpu.VMEM((rows, WIDTH_A), BF16),
            pltpu.VMEM((rows, 1), F32),
            pltpu.VMEM((rows, WIDTH_A), F32),
            pltpu.VMEM((page, WIDTH_A), F32),
            pltpu.VMEM((page, WIDTH_A), F32),
        ],
    )
    return pl.pallas_call(
        functools.partial(_attn_decode_kernel, n_new=n_new, page=page, pps=pps),
        grid_spec=grid_spec,
        out_shape=jax.ShapeDtypeStruct((dec_b * n_new, WIDTH_A), F32),
        compiler_params=_cparams(("parallel", "arbitrary")),
        name="sb_attn_decode",
    )(page_table, bias_rows, slots, k_new, v_new, *([cache_kt] * pps), *([cache_vt] * pps))


def _hgrn_kernel(lb_ref, ng_ref, q_ref, f_ref, i_ref, g_ref, s0_ref, o_ref, s_ref,
                 st_scr, b_scr, q_scr, k_scr, o_scr, *, chunk, sub):
    c = pl.program_id(1)

    @pl.when(c == 0)
    def _():
        st_scr[...] = s0_ref[...]

    lb = lb_ref[...]
    zf = f_ref[...]
    a1 = jnp.log(lb)
    a2 = jnp.log1p(-lb) - _softplus(-zf)
    logf = jnp.maximum(a1, a2) + jnp.log(1.0 + jnp.exp(-jnp.abs(a1 - a2)))
    k_scr[...] = (1.0 - lb) * jax.nn.sigmoid(-zf)
    qv = q_ref[...]
    q_scr[...] = qv * jax.nn.sigmoid(qv)
    r = lax.broadcasted_iota(jnp.int32, (chunk, chunk), 0)
    s = lax.broadcasted_iota(jnp.int32, (chunk, chunk), 1)
    lmat = jnp.where((s <= r) & (s // sub == r // sub), 1.0, 0.0).astype(BF16)
    h1, h2, h3 = _split3(logf)
    b_scr[...] = (_dot(lmat, h1) + _dot(lmat, h2) + _dot(lmat, h3)) * LOG2E

    grp = math.gcd(sub, SUBLANES)
    ngrp = sub // grp
    gidx = lax.broadcasted_iota(jnp.int32, (grp, 1), 0)

    states = [st_scr[h] for h in range(N_HEADS_B)]
    for ci in range(chunk // sub):
        rows = slice(ci * sub, (ci + 1) * sub)
        heads = []
        for h in range(N_HEADS_B):
            cols = slice(h * DK_B, (h + 1) * DK_B)
            b = b_scr[rows, cols]
            qs = q_scr[rows, cols]
            kk = k_scr[rows, cols]
            vv = i_ref[rows, cols]
            b_last = b[sub - 1:sub, :]
            st = states[h]
            o_state = _dot_nt((qs * jnp.exp2(b)).astype(BF16), st.astype(BF16))
            ke = kk * jnp.exp2(b_last - b)
            states[h] = st * jnp.exp2(b_last) + _dot_tn(vv.astype(BF16), ke.astype(BF16))
            heads.append((cols, b, qs, kk, vv, o_state))
        for cols, b, qs, kk, vv, o_state in heads:
            og = [None] * ngrp
            bg = [b[g * grp:(g + 1) * grp] for g in range(ngrp)]
            qg = [qs[g * grp:(g + 1) * grp] for g in range(ngrp)]
            for t in range(sub):
                bt = b[t:t + 1, :]
                kt = kk[t:t + 1, :]
                vt = vv[t:t + 1, :]
                for g in range(t // grp, ngrp):
                    if g == t // grp:
                        w = jnp.exp2(jnp.minimum(bg[g] - bt, 0.0))
                        p = jnp.sum(qg[g] * kt * w, axis=-1, keepdims=True)
                        p = jnp.where(gidx >= t - g * grp, p, 0.0)
                    else:
                        p = jnp.sum(qg[g] * kt * jnp.exp2(bg[g] - bt), axis=-1, keepdims=True)
                    og[g] = p * vt if og[g] is None else og[g] + p * vt
            o_scr[rows, cols] = (og[0] if ngrp == 1 else jnp.concatenate(og, axis=0)) + o_state
    for h in range(N_HEADS_B):
        st_scr[h] = states[h]

    gate = g_ref[...]
    gate = gate * jax.nn.sigmoid(gate)
    ng = ng_ref[...]
    for h in range(N_HEADS_B):
        cols = slice(h * DV_B, (h + 1) * DV_B)
        o = o_scr[:, cols]
        o = o * lax.rsqrt(jnp.mean(o * o, axis=-1, keepdims=True) + RMS_EPS) * ng
        o_ref[:, cols] = (o * gate[:, cols]).astype(o_ref.dtype)

    @pl.when(c == pl.num_programs(1) - 1)
    def _():
        s_ref[...] = st_scr[...]


def _hgrn(slots, lb_row, norm_g_row, s0_t, batch, seq):
    chunk = min(128, seq)
    sub = min(SUBCHUNK_B, chunk)
    nc = seq // chunk
    n = batch * seq
    blk = lambda sl: pl.BlockSpec((None, chunk, COL), lambda b, c: (sl, b * nc + c, 0))
    st_spec = pl.BlockSpec((None, N_HEADS_B, DV_B, DK_B), lambda b, c: (b, 0, 0, 0))
    return pl.pallas_call(
        functools.partial(_hgrn_kernel, chunk=chunk, sub=sub),
        grid=(batch, nc),
        in_specs=[
            pl.BlockSpec((1, WIDTH_B), lambda b, c: (0, 0)),
            pl.BlockSpec((1, DV_B), lambda b, c: (0, 0)),
            blk(SL_QB), blk(SL_FB), blk(SL_IB), blk(SL_GB),
            st_spec,
        ],
        out_specs=[
            pl.BlockSpec((chunk, COL), lambda b, c: (b * nc + c, 0)),
            st_spec,
        ],
        out_shape=[
            jax.ShapeDtypeStruct((n, COL), _branch_dtype(chunk)),
            jax.ShapeDtypeStruct((batch, N_HEADS_B, DV_B, DK_B), F32),
        ],
        scratch_shapes=[
            pltpu.VMEM((N_HEADS_B, DV_B, DK_B), F32),
            pltpu.VMEM((chunk, WIDTH_B), F32),
            pltpu.VMEM((chunk, WIDTH_B), F32),
            pltpu.VMEM((chunk, WIDTH_B), F32),
            pltpu.VMEM((chunk, COL), F32),
        ],
        compiler_params=_cparams(("parallel", "arbitrary")),
        name="hgrn2",
    )(lb_row, norm_g_row, slots, slots, slots, slots, s0_t)


def _merge_ln_kernel(x_ref, oa_ref, ob_ref, za0_ref, za1_ref, zb0_ref, zb1_ref, bm_ref,
                     woa_ref, wob_ref, wout_ref, g_ref, b_ref, o_ref, *, alpha):
    ya = _dot(oa_ref[...].astype(BF16), woa_ref[...])
    yb = _dot(ob_ref[...].astype(BF16), wob_ref[...])
    za = jnp.concatenate([za0_ref[...], za1_ref[...]], axis=-1)
    zb = jnp.concatenate([zb0_ref[...], zb1_ref[...]], axis=-1)
    bm = bm_ref[...]
    m = jax.nn.sigmoid(za + bm[0:1, :]) * ya + jax.nn.sigmoid(zb + bm[1:2, :]) * yb
    mix = _dot(m.astype(BF16), wout_ref[...])
    o_ref[...] = _layer_norm(alpha * x_ref[...] + mix, g_ref[...], b_ref[...])


def _merge_ln(x, oa, ob, slots, b_merge, w_oa, w_ob, w_out, ln_g, ln_b, layer, alpha):
    n = x.shape[0]
    tm = min(512, n)
    row = lambda w: pl.BlockSpec((tm, w), lambda i: (i, 0))
    zblk = lambda sl: pl.BlockSpec((None, tm, COL), lambda i: (sl, i, 0))
    wspec = lambda k: pl.BlockSpec((None, k, D_MODEL), lambda i: (layer, 0, 0))
    ln_spec = pl.BlockSpec((None, None, 1, D_MODEL), lambda i: (layer, 1, 0, 0))
    return pl.pallas_call(
        functools.partial(_merge_ln_kernel, alpha=alpha),
        grid=(n // tm,),
        in_specs=[
            row(D_MODEL), row(WIDTH_A), row(COL),
            zblk(SL_ZA), zblk(SL_ZA + 1), zblk(SL_ZB), zblk(SL_ZB + 1),
            pl.BlockSpec((None, 2, D_MODEL), lambda i: (layer, 0, 0)),
            wspec(WIDTH_A), wspec(COL), wspec(D_MODEL),
            ln_spec, ln_spec,
        ],
        out_specs=row(D_MODEL),
        out_shape=jax.ShapeDtypeStruct((n, D_MODEL), F32),
        compiler_params=_cparams(("parallel",)),
        name="merge_ln",
    )(x, oa, ob, slots, slots, slots, slots, b_merge, w_oa, w_ob, w_out, ln_g, ln_b)


def _embed_ln_kernel(x_ref, p_ref, wg_ref, wp_ref, bg_ref, g_ref, b_ref, o_ref, *, alpha):
    x = x_ref[...]
    gate = jax.nn.sigmoid(_dot(x.astype(BF16), wg_ref[...]) + bg_ref[...])
    e = gate * _dot(p_ref[...].astype(BF16), wp_ref[...])
    o_ref[...] = _layer_norm(alpha * x + e, g_ref[...], b_ref[...])


def _embed_ln(x, p, pe_w_gate, pe_w_proj, pe_b_gate, ln_g, ln_b, layer, alpha):
    n = x.shape[0]
    tm = min(512, n)
    ln_spec = pl.BlockSpec((None, None, 1, D_MODEL), lambda i: (layer, 3, 0, 0))
    return pl.pallas_call(
        functools.partial(_embed_ln_kernel, alpha=alpha),
        grid=(n // tm,),
        in_specs=[
            pl.BlockSpec((tm, D_MODEL), lambda i: (i, 0)),
            pl.BlockSpec((None, tm, P_DIM), lambda i: (layer, i, 0)),
            pl.BlockSpec((None, D_MODEL, D_MODEL), lambda i: (layer, 0, 0)),
            pl.BlockSpec((None, P_DIM, D_MODEL), lambda i: (layer, 0, 0)),
            pl.BlockSpec((None, 1, D_MODEL), lambda i: (layer, 0, 0)),
            ln_spec, ln_spec,
        ],
        out_specs=pl.BlockSpec((tm, D_MODEL), lambda i: (i, 0)),
        out_shape=jax.ShapeDtypeStruct((n, D_MODEL), F32),
        compiler_params=_cparams(("parallel",)),
        name="embed_ln",
    )(x, p, pe_w_gate, pe_w_proj, pe_b_gate, ln_g, ln_b)


def kernel(x_prompt, x_sample, cache_k, cache_v, state_hgrn, page_table, p_prompt, p_sample, ln_g, ln_b, ffn_w_gate, ffn_w_up, ffn_w_down, w_in, b_merge, sb_bias, hgrn_lb, hgrn_norm_g, w_oa, w_ob, w_out, pe_w_proj, pe_w_gate, pe_b_gate):
    depth = ln_g.shape[0]
    alpha = (2.0 * depth) ** 0.25
    batch, seq, _ = x_prompt.shape
    dec_b, dec_seq, _ = x_sample.shape
    n_pool, page = cache_k.shape[1], cache_k.shape[2]

    lb_all = jnp.cumsum(jax.nn.softmax(hgrn_lb.astype(F32), axis=0), axis=0)
    lb_all = lb_all - lb_all[0:1]

    wg = ffn_w_gate.astype(BF16)
    wu = ffn_w_up.astype(BF16)
    wd = ffn_w_down.astype(BF16)
    w_in_b = w_in.astype(BF16)
    w_kv_t = jnp.swapaxes(w_in_b[:, :, COL:3 * COL], 1, 2)
    w_oa_b = w_oa.astype(BF16)
    w_ob_b = w_ob.astype(BF16)
    w_out_b = w_out.astype(BF16)
    pe_wg_b = pe_w_gate.astype(BF16)
    pe_wp_b = pe_w_proj.astype(BF16)
    ln_g4 = ln_g.reshape(depth, 4, 1, D_MODEL)
    ln_b4 = ln_b.reshape(depth, 4, 1, D_MODEL)
    pe_bg3 = pe_b_gate.reshape(depth, 1, D_MODEL)
    cache_kt = jnp.transpose(cache_k, (0, 1, 3, 4, 2)).reshape(depth, n_pool, WIDTH_A, page)
    cache_vt = jnp.transpose(cache_v, (0, 1, 3, 4, 2)).reshape(depth, n_pool, WIDTH_A, page)
    p_prompt3 = p_prompt.reshape(depth, batch * seq, P_DIM)
    p_sample3 = p_sample.reshape(depth, dec_b * dec_seq, P_DIM)
    state_t = jnp.swapaxes(state_hgrn, -1, -2)
    zero_state = jnp.zeros((batch, N_HEADS_B, DV_B, DK_B), F32)

    def layer_tail(x, p3, i, slots, oa, ob):
        x = _merge_ln(x, oa, ob, slots, b_merge, w_oa_b, w_ob_b, w_out_b, ln_g4, ln_b4, i, alpha)
        x = _ffn_ln(x, wg, wu, wd, ln_g4, ln_b4, i, 1, 2, alpha)
        return _embed_ln(x, p3, pe_wg_b, pe_wp_b, pe_bg3, ln_g4, ln_b4, i, alpha)

    yp = x_prompt.reshape(batch * seq, D_MODEL)
    ys = x_sample.reshape(dec_b * dec_seq, D_MODEL)
    kt_all = vt_all = None
    sp, kd, vd, sd = [], [], [], []
    for i in range(depth):
        lb_row = lb_all[i][None, :]
        ng_row = hgrn_norm_g[i][None, :]
        yp = _ffn_ln(yp, wg, wu, wd, ln_g4, ln_b4, i, 0, 0, alpha)
        slots, kt_all, vt_all = _in_proj_prompt(yp, w_in_b, w_kv_t, i, depth, batch, seq, kt_all, vt_all)
        oa = _attn_prompt(slots, kt_all, vt_all, sb_bias[i], i, batch, seq)
        ob, s_t = _hgrn(slots, lb_row, ng_row, zero_state, batch, seq)
        yp = layer_tail(yp, p_prompt3, i, slots, oa, ob)
        sp.append(jnp.swapaxes(s_t, -1, -2))
        ys = _ffn_ln(ys, wg, wu, wd, ln_g4, ln_b4, i, 0, 0, alpha)
        slots, k_new, v_new = _in_proj_rows(ys, w_in_b, i)
        oa = _attn_decode(slots, k_new, v_new, cache_kt, cache_vt, page_table, sb_bias[i], i,
                          dec_b, dec_seq)
        ob, s_t = _hgrn(slots, lb_row, ng_row, state_t[i], dec_b, dec_seq)
        ys = layer_tail(ys, p_sample3, i, slots, oa, ob)
        kd.append(k_new.reshape(dec_b, dec_seq, N_HEADS_A, HEAD_DIM_A))
        vd.append(v_new.reshape(dec_b, dec_seq, N_HEADS_A, HEAD_DIM_A))
        sd.append(jnp.swapaxes(s_t, -1, -2))

    def token_major(t_all):
        t5 = t_all.reshape(depth, batch, N_HEADS_A, HEAD_DIM_A, seq)
        return jnp.transpose(t5, (0, 1, 4, 2, 3))

    return (yp.reshape(batch, seq, D_MODEL), ys.reshape(dec_b, dec_seq, D_MODEL),
            token_major(kt_all), token_major(vt_all), jnp.stack(sp),
            jnp.stack(kd), jnp.stack(vd), jnp.stack(sd))
```

```python
import functools
import math

import jax
import jax.numpy as jnp
from jax import lax
from jax.experimental import pallas as pl
from jax.experimental.pallas import tpu as pltpu

F32 = jnp.float32
BF16 = jnp.bfloat16

D_MODEL = 1024
N_HEADS_A = 8
HEAD_DIM_A = 64
WIDTH_A = N_HEADS_A * HEAD_DIM_A
N_HEADS_B = 4
DK_B = 128
DV_B = 128
WIDTH_B = N_HEADS_B * DK_B
D_FF = 2816
P_DIM = 256
LN_EPS = 1e-5
RMS_EPS = 1e-6
LOG2E = 1.4426950408889634
COL = 512
N_COLBLK = 11
N_SLOTS = N_COLBLK - 2
(SL_QA, SL_QB, SL_FB, SL_IB, SL_GB, SL_ZA, SL_ZB) = (0, 1, 2, 3, 4, 5, 7)

LANES = 128
SUBLANES = 8
BF16_SUBLANES = 16
MXU_DIM = 256
VMEM_LIMIT_BYTES = 56 * 1024 * 1024
FFN_TILE_F = D_FF // 2
FFN_TILE_ROWS = 1024
IN_PROJ_TILE_ROWS = 512
SUBCHUNK_B = 16
PAGES_PER_STEP = 32
ATTN_Q_TILE = 1024
ATTN_ROW_CHUNK = 256
ATTN_KEY_CHUNKS = 2


def _cparams(sem):
    return pltpu.CompilerParams(dimension_semantics=sem, vmem_limit_bytes=VMEM_LIMIT_BYTES)


def _branch_dtype(block_rows):
    return BF16 if block_rows % BF16_SUBLANES == 0 else F32


def _dot(a, b):
    return jnp.dot(a, b, preferred_element_type=F32)


def _dot_nt(a, b):
    return lax.dot_general(a, b, (((1,), (1,)), ((), ())), preferred_element_type=F32)


def _dot_tn(a, b):
    return lax.dot_general(a, b, (((0,), (0,)), ((), ())), preferred_element_type=F32)


def _split2(x):
    hi = x.astype(BF16)
    lo = (x - hi.astype(F32)).astype(BF16)
    return hi, lo


def _split3(x):
    h1 = x.astype(BF16)
    r = x - h1.astype(F32)
    h2 = r.astype(BF16)
    h3 = (r - h2.astype(F32)).astype(BF16)
    return h1, h2, h3


def _softplus(z):
    return jnp.maximum(z, 0.0) + jnp.log(1.0 + jnp.exp(-jnp.abs(z)))


def _softplus2(z2):
    return jnp.maximum(z2, 0.0) + jnp.log(1.0 + jnp.exp2(-jnp.abs(z2))) * LOG2E


def _layer_norm(y, g, b):
    mu = jnp.mean(y, axis=-1, keepdims=True)
    yc = y - mu
    var = jnp.mean(yc * yc, axis=-1, keepdims=True)
    return yc * lax.rsqrt(var + LN_EPS) * g + b


def _suffix_matrix(n, halves):
    row = lax.broadcasted_iota(jnp.int32, (halves * n, n), 0)
    col = lax.broadcasted_iota(jnp.int32, (halves * n, n), 1)
    j = jnp.where(row >= n, row - n, row)
    return jnp.where((j >= col) if halves == 2 else (j > col), 1.0, 0.0).astype(BF16)


def _ffn_ln_kernel(x_ref, wg_ref, wu_ref, wd_ref, g_ref, b_ref, o_ref, *, alpha, sub_rows, tile_f):
    for s in range(x_ref.shape[0] // sub_rows):
        rows = slice(s * sub_rows, (s + 1) * sub_rows)
        x = x_ref[rows, :]
        xb = x.astype(BF16)
        acc = None
        for f in range(D_FF // tile_f):
            cols = slice(f * tile_f, (f + 1) * tile_f)
            g = _dot(xb, wg_ref[:, cols])
            u = _dot(xb, wu_ref[:, cols])
            h = (g * jax.nn.sigmoid(g) * u).astype(BF16)
            part = _dot(h, wd_ref[cols, :])
            acc = part if acc is None else acc + part
        o_ref[rows, :] = _layer_norm(alpha * x + 0.5 * acc, g_ref[...], b_ref[...])


def _ffn_ln(x, wg, wu, wd, ln_g, ln_b, layer, which, ln_idx, alpha):
    n = x.shape[0]
    tm = min(FFN_TILE_ROWS, n)
    sub_rows = tm // 2 if tm % (2 * BF16_SUBLANES) == 0 else tm
    resident = pl.Buffered(1)
    return pl.pallas_call(
        functools.partial(_ffn_ln_kernel, alpha=alpha, sub_rows=sub_rows, tile_f=FFN_TILE_F),
        grid=(n // tm,),
        in_specs=[
            pl.BlockSpec((tm, D_MODEL), lambda i: (i, 0)),
            pl.BlockSpec((None, None, D_MODEL, D_FF), lambda i: (layer, which, 0, 0), resident),
            pl.BlockSpec((None, None, D_MODEL, D_FF), lambda i: (layer, which, 0, 0), resident),
            pl.BlockSpec((None, None, D_FF, D_MODEL), lambda i: (layer, which, 0, 0), resident),
            pl.BlockSpec((None, None, 1, D_MODEL), lambda i: (layer, ln_idx, 0, 0)),
            pl.BlockSpec((None, None, 1, D_MODEL), lambda i: (layer, ln_idx, 0, 0)),
        ],
        out_specs=pl.BlockSpec((tm, D_MODEL), lambda i: (i, 0)),
        out_shape=jax.ShapeDtypeStruct((n, D_MODEL), F32),
        compiler_params=_cparams(("parallel",)),
        name="ffn_ln",
    )(x, wg, wu, wd, ln_g, ln_b)


def _in_proj_kernel(x_ref, w_ref, *rest, feature_major, n_alias):
    if feature_major:
        wkv_ref = rest[0]
        rest = rest[1:]
    o_ref, k_ref, v_ref = rest[n_alias:]
    xb = x_ref[...].astype(BF16)
    for j in range(N_COLBLK):
        if j in (1, 2):
            ref = k_ref if j == 1 else v_ref
            if feature_major:
                ref[...] = _dot_nt(wkv_ref[(j - 1) * COL:j * COL, :], xb)
            else:
                ref[...] = _dot(xb, w_ref[:, j * COL:(j + 1) * COL])
        else:
            o_ref[0 if j == 0 else j - 2] = _dot(xb, w_ref[:, j * COL:(j + 1) * COL])


def _in_proj_rows(x, w_in, layer):
    n = x.shape[0]
    tm = min(IN_PROJ_TILE_ROWS, n)
    kv_spec = pl.BlockSpec((tm, COL), lambda i: (i, 0))
    return pl.pallas_call(
        functools.partial(_in_proj_kernel, feature_major=False, n_alias=0),
        grid=(n // tm,),
        in_specs=[
            pl.BlockSpec((tm, D_MODEL), lambda i: (i, 0)),
            pl.BlockSpec((None, D_MODEL, N_COLBLK * COL), lambda i: (layer, 0, 0), pl.Buffered(1)),
        ],
        out_specs=[
            pl.BlockSpec((N_SLOTS, tm, COL), lambda i: (0, i, 0)),
            kv_spec, kv_spec,
        ],
        out_shape=[
            jax.ShapeDtypeStruct((N_SLOTS, n, COL), F32),
            jax.ShapeDtypeStruct((n, COL), F32),
            jax.ShapeDtypeStruct((n, COL), F32),
        ],
        compiler_params=_cparams(("parallel",)),
        name="in_proj_rows",
    )(x, w_in)


def _in_proj_prompt(x, w_in, w_kv_t, layer, depth, batch, seq, kt_prev, vt_prev):
    n = x.shape[0]
    tm = min(IN_PROJ_TILE_ROWS, seq)
    nt = seq // tm
    n_alias = 0 if kt_prev is None else 2
    kv_spec = pl.BlockSpec((None, None, COL, tm), lambda i: (layer, i // nt, 0, i % nt))
    resident = pl.Buffered(1)
    in_specs = [
        pl.BlockSpec((tm, D_MODEL), lambda i: (i, 0)),
        pl.BlockSpec((None, D_MODEL, N_COLBLK * COL), lambda i: (layer, 0, 0), resident),
        pl.BlockSpec((None, 2 * COL, D_MODEL), lambda i: (layer, 0, 0), resident),
    ]
    args = [x, w_in, w_kv_t]
    aliases = {}
    if n_alias:
        in_specs += [pl.BlockSpec(memory_space=pl.ANY)] * 2
        args += [kt_prev, vt_prev]
        aliases = {3: 1, 4: 2}
    kv_shape = jax.ShapeDtypeStruct((depth, batch, COL, seq), F32)
    return pl.pallas_call(
        functools.partial(_in_proj_kernel, feature_major=True, n_alias=n_alias),
        grid=(n // tm,),
        in_specs=in_specs,
        out_specs=[
            pl.BlockSpec((N_SLOTS, tm, COL), lambda i: (0, i, 0)),
            kv_spec, kv_spec,
        ],
        out_shape=[jax.ShapeDtypeStruct((N_SLOTS, n, COL), F32), kv_shape, kv_shape],
        input_output_aliases=aliases,
        compiler_params=_cparams(("parallel",)),
        name="in_proj_prompt",
    )(*args)


def _sb_suffix(zs, suffix_m, causals):
    blk = suffix_m.shape[1]
    nblk = zs[0].shape[1] // blk
    nls = [_softplus2(z) for z in zs]
    nls = [nl if m is None else jnp.where(m, nl, 0.0) for nl, m in zip(nls, causals)]

    def stacked_split(nl):
        halves = _split2(nl) if suffix_m.shape[0] == 2 * blk else (nl.astype(BF16),)
        return jnp.concatenate(
            [jnp.concatenate([h[:, j * blk:(j + 1) * blk] for h in halves], axis=1)
             for j in range(nblk)], axis=0)

    return nls, [_dot(stacked_split(nl), suffix_m) for nl in nls]


def _sb_finish(zs, nls, suffix, cs, causals, inclusive, chained=False):
    blk = suffix[0].shape[1]
    nblk = zs[0].shape[1] // blk
    rows = zs[0].shape[0]
    out_a, out_c = [], []
    c = None
    for z, nl, suf, c_in, m in zip(zs, nls, suffix, cs, causals):
        c = c_in if (c is None or not chained) else c
        base = z if inclusive else z - nl
        parts = [None] * nblk
        for j in reversed(range(nblk)):
            sl = slice(j * blk, (j + 1) * blk)
            parts[j] = jnp.exp2(base[:, sl] - suf[j * rows:(j + 1) * rows, :] - c)
            c = c + jnp.sum(nl[:, sl], axis=-1, keepdims=True)
        a = parts[0] if nblk == 1 else jnp.concatenate(parts, axis=1)
        out_a.append(a if m is None else jnp.where(m, a, 0.0))
        out_c.append(c)
    return out_a, out_c


def _attn_prompt_kernel(bias_ref, q_ref, kt_ref, vt_ref, o_ref, *, tile, row_chunk, key_chunks):
    pair = pl.program_id(1)
    qi = pl.program_id(2)
    q = q_ref[...] * (LOG2E / math.sqrt(HEAD_DIM_A))
    lane = lax.broadcasted_iota(jnp.int32, (tile, LANES), 1)
    q0 = jnp.where(lane < HEAD_DIM_A, q, 0.0)
    q_heads = (q0.astype(BF16), (q - q0).astype(BF16))
    biases = (bias_ref[2 * pair] * LOG2E, bias_ref[2 * pair + 1] * LOG2E)
    rc = min(row_chunk, tile)
    nrc = tile // rc
    kfac = math.gcd(key_chunks, nrc)
    kb = kfac * rc
    chains = [(hh, r) for hh in range(2) for r in range(nrc)]
    q_chain = [q_heads[hh][r * rc:(r + 1) * rc] for hh, r in chains]
    row = lax.broadcasted_iota(jnp.int32, (rc, kb), 0)
    col = lax.broadcasted_iota(jnp.int32, (rc, kb), 1)
    suffix_m = _suffix_matrix(MXU_DIM if rc % MXU_DIM == 0 else rc, 1)
    feat = lax.broadcasted_iota(jnp.int32, (LANES, kb), 0)
    head0_feat = feat < HEAD_DIM_A

    def step(kj, cs, ots, diag):
        start = pl.multiple_of(kj * kb, kb)
        kt = kt_ref[:, pl.ds(start, kb)].astype(BF16)
        vt = vt_ref[:, pl.ds(start, kb)]
        vt_heads = (jnp.where(head0_feat, vt, 0.0).astype(BF16),
                    jnp.where(head0_feat, 0.0, vt).astype(BF16))
        first = 0 if diag is None else diag * kfac
        live = [i for i, (_, r) in enumerate(chains) if r >= first]
        masks = [col < row + (chains[i][1] - first) * rc
                 if diag is not None and chains[i][1] < first + kfac else None for i in live]
        zs = [_dot(q_chain[i], kt) + biases[chains[i][0]] for i in live]
        nls, suffix = _sb_suffix(zs, suffix_m, masks)
        a_s, c_new = _sb_finish(zs, nls, suffix, [cs[i] for i in live], masks, inclusive=False)
        cs, ots = list(cs), list(ots)
        for i, a, c in zip(live, a_s, c_new):
            hh, r = chains[i]
            cs[i] = c
            ots[r] = ots[r] + _dot_nt(vt_heads[hh], a.astype(BF16))
        return tuple(cs), tuple(ots)

    cs = tuple(jnp.zeros((rc, 1), F32) for _ in chains)
    ots = tuple(jnp.zeros((LANES, rc), F32) for _ in range(nrc))
    n_diag = nrc // kfac
    for d in reversed(range(n_diag)):
        cs, ots = step(qi * n_diag + d, cs, ots, d)
    def below(jj, co):
        for u in range(n_diag):
            co = step((qi - jj) * n_diag - 1 - u, co[0], co[1], None)
        return co

    cs, ots = lax.fori_loop(0, qi, below, (cs, ots))
    for r in range(nrc):
        o_ref[r * rc:(r + 1) * rc, :] = ots[r].T.astype(o_ref.dtype)


def _attn_prompt(slots, kt_all, vt_all, sb_bias, layer, batch, seq):
    tile = min(ATTN_Q_TILE, seq)
    nq = seq // tile
    n = batch * seq
    kv_spec = pl.BlockSpec((None, None, LANES, seq), lambda b, p, i: (layer, b, p, 0))
    return pl.pallas_call(
        functools.partial(_attn_prompt_kernel, tile=tile, row_chunk=ATTN_ROW_CHUNK,
                          key_chunks=ATTN_KEY_CHUNKS),
        grid=(batch, WIDTH_A // LANES, nq),
        in_specs=[
            pl.BlockSpec(memory_space=pltpu.SMEM),
            pl.BlockSpec((None, tile, LANES), lambda b, p, i: (SL_QA, b * nq + i, p)),
            kv_spec, kv_spec,
        ],
        out_specs=pl.BlockSpec((tile, LANES), lambda b, p, i: (b * nq + i, p)),
        out_shape=jax.ShapeDtypeStruct((n, WIDTH_A), _branch_dtype(tile)),
        compiler_params=_cparams(("parallel", "parallel", "arbitrary")),
        name="sb_attn_prompt",
    )(sb_bias, slots, kt_all, vt_all)


def _attn_decode_kernel(pt_ref, bias_ref, q_ref, kn_ref, vn_ref, *rest, n_new, page, pps):
    k_refs = rest[:pps]
    v_refs = rest[pps:2 * pps]
    o_ref = rest[2 * pps]
    qbd_scr, c_scr, acc_scr, kpad_scr, vpad_scr = rest[2 * pps + 1:]
    g = pl.program_id(1)
    rows = N_HEADS_A * n_new
    bias = bias_ref[...] * LOG2E

    @pl.when(g == 0)
    def _():
        q = q_ref[...] * (LOG2E / math.sqrt(HEAD_DIM_A))
        qt = jnp.concatenate([q] * N_HEADS_A, axis=0)
        r = lax.broadcasted_iota(jnp.int32, (rows, WIDTH_A), 0)
        f = lax.broadcasted_iota(jnp.int32, (rows, WIDTH_A), 1)
        qbd = jnp.where(r // n_new == f // HEAD_DIM_A, qt, 0.0).astype(BF16)
        qbd_scr[...] = qbd
        kpad_scr[...] = jnp.zeros_like(kpad_scr)
        vpad_scr[...] = jnp.zeros_like(vpad_scr)
        kpad_scr[0:n_new, :] = kn_ref[...]
        vpad_scr[0:n_new, :] = vn_ref[...]
        rr = lax.broadcasted_iota(jnp.int32, (rows, page), 0)
        ss = lax.broadcasted_iota(jnp.int32, (rows, page), 1)
        causal = ss < (rr % n_new)
        z = _dot_nt(qbd, kpad_scr[...].astype(BF16)) + bias
        nls, suffix = _sb_suffix([z], _suffix_matrix(page, 2), [causal])
        (a,), (c,) = _sb_finish([z], nls, suffix, [jnp.zeros((rows, 1), F32)], [causal], inclusive=True)
        acc_scr[...] = _dot(a.astype(BF16), vpad_scr[...].astype(BF16))
        c_scr[...] = c

    kt = jnp.concatenate([r[...] for r in k_refs], axis=1).astype(BF16)
    vt = jnp.concatenate([r[...] for r in v_refs], axis=1).astype(BF16)
    z = _dot(qbd_scr[...], kt) + bias
    blk = math.gcd(MXU_DIM, pps * page)
    nls, suffix = _sb_suffix([z], _suffix_matrix(blk, 2), [None])
    (a,), (c,) = _sb_finish([z], nls, suffix, [c_scr[...]], [None], inclusive=True)
    acc_scr[...] += _dot_nt(a.astype(BF16), vt)
    c_scr[...] = c

    @pl.when(g == pl.num_programs(1) - 1)
    def _():
        acc = acc_scr[...]
        f = lax.broadcasted_iota(jnp.int32, (n_new, WIDTH_A), 1)
        out = jnp.zeros((n_new, WIDTH_A), F32)
        for h in range(N_HEADS_A):
            out = out + jnp.where(f // HEAD_DIM_A == h, acc[h * n_new:(h + 1) * n_new, :], 0.0)
        o_ref[...] = out


def _attn_decode(slots, k_new, v_new, cache_kt, cache_vt, page_table, sb_bias, layer, dec_b, n_new):
    page = cache_kt.shape[3]
    n_pages = page_table.shape[1]
    pps = math.gcd(PAGES_PER_STEP, n_pages)
    steps = n_pages // pps
    rows = N_HEADS_A * n_new
    bias_rows = jnp.repeat(sb_bias.astype(F32), n_new)[:, None]

    def page_spec(i):
        return pl.BlockSpec(
            (None, None, WIDTH_A, page),
            lambda b, g, pt: (layer, pt[b, n_pages - (g + 1) * pps + i], 0, 0))

    new_spec = pl.BlockSpec((n_new, COL), lambda b, g, pt: (b, 0))
    grid_spec = pltpu.PrefetchScalarGridSpec(
        num_scalar_prefetch=1,
        grid=(dec_b, steps),
        in_specs=[
            pl.BlockSpec((rows, 1), lambda b, g, pt: (0, 0)),
            pl.BlockSpec((None, n_new, COL), lambda b, g, pt: (SL_QA, b, 0)),
            new_spec, new_spec,
        ] + [page_spec(i) for i in range(pps)] + [page_spec(i) for i in range(pps)],
        out_specs=pl.BlockSpec((n_new, WIDTH_A), lambda b, g, pt: (b, 0)),
        scratch_shapes=[
            pltpu.VMEM((rows, WIDTH_A), BF16),
            pltpu.VMEM((rows, 1), F32),
            pltpu.VMEM((rows, WIDTH_A), F32),
            pltpu.VMEM((page, WIDTH_A), F32),
            pltpu.VMEM((page, WIDTH_A), F32),
        ],
    )
    return pl.pallas_call(
        functools.partial(_attn_decode_kernel, n_new=n_new, page=page, pps=pps),
        grid_spec=grid_spec,
        out_shape=jax.ShapeDtypeStruct((dec_b * n_new, WIDTH_A), F32),
        compiler_params=_cparams(("parallel", "arbitrary")),
        name="sb_attn_decode",
    )(page_table, bias_rows, slots, k_new, v_new, *([cache_kt] * pps), *([cache_vt] * pps))


def _hgrn_kernel(lb_ref, ng_ref, q_ref, f_ref, i_ref, g_ref, s0_ref, o_ref, s_ref,
                 st_scr, b_scr, q_scr, k_scr, o_scr, *, chunk, sub):
    c = pl.program_id(1)

    @pl.when(c == 0)
    def _():
        st_scr[...] = s0_ref[...]

    lb = lb_ref[...]
    zf = f_ref[...]
    a1 = jnp.log(lb)
    a2 = jnp.log1p(-lb) - _softplus(-zf)
    logf = jnp.maximum(a1, a2) + jnp.log(1.0 + jnp.exp(-jnp.abs(a1 - a2)))
    k_scr[...] = (1.0 - lb) * jax.nn.sigmoid(-zf)
    qv = q_ref[...]
    q_scr[...] = qv * jax.nn.sigmoid(qv)
    r = lax.broadcasted_iota(jnp.int32, (chunk, chunk), 0)
    s = lax.broadcasted_iota(jnp.int32, (chunk, chunk), 1)
    lmat = jnp.where((s <= r) & (s // sub == r // sub), 1.0, 0.0).astype(BF16)
    h1, h2, h3 = _split3(logf)
    b_scr[...] = (_dot(lmat, h1) + _dot(lmat, h2) + _dot(lmat, h3)) * LOG2E

    grp = math.gcd(sub, SUBLANES)
    ngrp = sub // grp
    gidx = lax.broadcasted_iota(jnp.int32, (grp, 1), 0)

    states = [st_scr[h] for h in range(N_HEADS_B)]
    for ci in range(chunk // sub):
        rows = slice(ci * sub, (ci + 1) * sub)
        heads = []
        for h in range(N_HEADS_B):
            cols = slice(h * DK_B, (h + 1) * DK_B)
            b = b_scr[rows, cols]
            qs = q_scr[rows, cols]
            kk = k_scr[rows, cols]
            vv = i_ref[rows, cols]
            b_last = b[sub - 1:sub, :]
            st = states[h]
            o_state = _dot_nt((qs * jnp.exp2(b)).astype(BF16), st.astype(BF16))
            ke = kk * jnp.exp2(b_last - b)
            states[h] = st * jnp.exp2(b_last) + _dot_tn(vv.astype(BF16), ke.astype(BF16))
            heads.append((cols, b, qs, kk, vv, o_state))
        for cols, b, qs, kk, vv, o_state in heads:
            og = [None] * ngrp
            bg = [b[g * grp:(g + 1) * grp] for g in range(ngrp)]
            qg = [qs[g * grp:(g + 1) * grp] for g in range(ngrp)]
            for t in range(sub):
                bt = b[t:t + 1, :]
                kt = kk[t:t + 1, :]
                vt = vv[t:t + 1, :]
                for g in range(t // grp, ngrp):
                    if g == t // grp:
                        w = jnp.exp2(jnp.minimum(bg[g] - bt, 0.0))
                        p = jnp.sum(qg[g] * kt * w, axis=-1, keepdims=True)
                        p = jnp.where(gidx >= t - g * grp, p, 0.0)
                    else:
                        p = jnp.sum(qg[g] * kt * jnp.exp2(bg[g] - bt), axis=-1, keepdims=True)
                    og[g] = p * vt if og[g] is None else og[g] + p * vt
            o_scr[rows, cols] = (og[0] if ngrp == 1 else jnp.concatenate(og, axis=0)) + o_state
    for h in range(N_HEADS_B):
        st_scr[h] = states[h]

    gate = g_ref[...]
    gate = gate * jax.nn.sigmoid(gate)
    ng = ng_ref[...]
    for h in range(N_HEADS_B):
        cols = slice(h * DV_B, (h + 1) * DV_B)
        o = o_scr[:, cols]
        o = o * lax.rsqrt(jnp.mean(o * o, axis=-1, keepdims=True) + RMS_EPS) * ng
        o_ref[:, cols] = (o * gate[:, cols]).astype(o_ref.dtype)

    @pl.when(c == pl.num_programs(1) - 1)
    def _():
        s_ref[...] = st_scr[...]


def _hgrn(slots, lb_row, norm_g_row, s0_t, batch, seq):
    chunk = min(128, seq)
    sub = min(SUBCHUNK_B, chunk)
    nc = seq // chunk
    n = batch * seq
    blk = lambda sl: pl.BlockSpec((None, chunk, COL), lambda b, c: (sl, b * nc + c, 0))
    st_spec = pl.BlockSpec((None, N_HEADS_B, DV_B, DK_B), lambda b, c: (b, 0, 0, 0))
    return pl.pallas_call(
        functools.partial(_hgrn_kernel, chunk=chunk, sub=sub),
        grid=(batch, nc),
        in_specs=[
            pl.BlockSpec((1, WIDTH_B), lambda b, c: (0, 0)),
            pl.BlockSpec((1, DV_B), lambda b, c: (0, 0)),
            blk(SL_QB), blk(SL_FB), blk(SL_IB), blk(SL_GB),
            st_spec,
        ],
        out_specs=[
            pl.BlockSpec((chunk, COL), lambda b, c: (b * nc + c, 0)),
            st_spec,
        ],
        out_shape=[
            jax.ShapeDtypeStruct((n, COL), _branch_dtype(chunk)),
            jax.ShapeDtypeStruct((batch, N_HEADS_B, DV_B, DK_B), F32),
        ],
        scratch_shapes=[
            pltpu.VMEM((N_HEADS_B, DV_B, DK_B), F32),
            pltpu.VMEM((chunk, WIDTH_B), F32),
            pltpu.VMEM((chunk, WIDTH_B), F32),
            pltpu.VMEM((chunk, WIDTH_B), F32),
            pltpu.VMEM((chunk, COL), F32),
        ],
        compiler_params=_cparams(("parallel", "arbitrary")),
        name="hgrn2",
    )(lb_row, norm_g_row, slots, slots, slots, slots, s0_t)


def _merge_ln_kernel(x_ref, oa_ref, ob_ref, za0_ref, za1_ref, zb0_ref, zb1_ref, bm_ref,
                     woa_ref, wob_ref, wout_ref, g_ref, b_ref, o_ref, *, alpha):
    ya = _dot(oa_ref[...].astype(BF16), woa_ref[...])
    yb = _dot(ob_ref[...].astype(BF16), wob_ref[...])
    za = jnp.concatenate([za0_ref[...], za1_ref[...]], axis=-1)
    zb = jnp.concatenate([zb0_ref[...], zb1_ref[...]], axis=-1)
    bm = bm_ref[...]
    m = jax.nn.sigmoid(za + bm[0:1, :]) * ya + jax.nn.sigmoid(zb + bm[1:2, :]) * yb
    mix = _dot(m.astype(BF16), wout_ref[...])
    o_ref[...] = _layer_norm(alpha * x_ref[...] + mix, g_ref[...], b_ref[...])


def _merge_ln(x, oa, ob, slots, b_merge, w_oa, w_ob, w_out, ln_g, ln_b, layer, alpha):
    n = x.shape[0]
    tm = min(512, n)
    row = lambda w: pl.BlockSpec((tm, w), lambda i: (i, 0))
    zblk = lambda sl: pl.BlockSpec((None, tm, COL), lambda i: (sl, i, 0))
    wspec = lambda k: pl.BlockSpec((None, k, D_MODEL), lambda i: (layer, 0, 0))
    ln_spec = pl.BlockSpec((None, None, 1, D_MODEL), lambda i: (layer, 1, 0, 0))
    return pl.pallas_call(
        functools.partial(_merge_ln_kernel, alpha=alpha),
        grid=(n // tm,),
        in_specs=[
            row(D_MODEL), row(WIDTH_A), row(COL),
            zblk(SL_ZA), zblk(SL_ZA + 1), zblk(SL_ZB), zblk(SL_ZB + 1),
            pl.BlockSpec((None, 2, D_MODEL), lambda i: (layer, 0, 0)),
            wspec(WIDTH_A), wspec(COL), wspec(D_MODEL),
            ln_spec, ln_spec,
        ],
        out_specs=row(D_MODEL),
        out_shape=jax.ShapeDtypeStruct((n, D_MODEL), F32),
        compiler_params=_cparams(("parallel",)),
        name="merge_ln",
    )(x, oa, ob, slots, slots, slots, slots, b_merge, w_oa, w_ob, w_out, ln_g, ln_b)


def _embed_ln_kernel(x_ref, p_ref, wg_ref, wp_ref, bg_ref, g_ref, b_ref, o_ref, *, alpha):
    x = x_ref[...]
    gate = jax.nn.sigmoid(_dot(x.astype(BF16), wg_ref[...]) + bg_ref[...])
    e = gate * _dot(p_ref[...].astype(BF16), wp_ref[...])
    o_ref[...] = _layer_norm(alpha * x + e, g_ref[...], b_ref[...])


def _embed_ln(x, p, pe_w_gate, pe_w_proj, pe_b_gate, ln_g, ln_b, layer, alpha):
    n = x.shape[0]
    tm = min(512, n)
    ln_spec = pl.BlockSpec((None, None, 1, D_MODEL), lambda i: (layer, 3, 0, 0))
    return pl.pallas_call(
        functools.partial(_embed_ln_kernel, alpha=alpha),
        grid=(n // tm,),
        in_specs=[
            pl.BlockSpec((tm, D_MODEL), lambda i: (i, 0)),
            pl.BlockSpec((None, tm, P_DIM), lambda i: (layer, i, 0)),
            pl.BlockSpec((None, D_MODEL, D_MODEL), lambda i: (layer, 0, 0)),
            pl.BlockSpec((None, P_DIM, D_MODEL), lambda i: (layer, 0, 0)),
            pl.BlockSpec((None, 1, D_MODEL), lambda i: (layer, 0, 0)),
            ln_spec, ln_spec,
        ],
        out_specs=pl.BlockSpec((tm, D_MODEL), lambda i: (i, 0)),
        out_shape=jax.ShapeDtypeStruct((n, D_MODEL), F32),
        compiler_params=_cparams(("parallel",)),
        name="embed_ln",
    )(x, p, pe_w_gate, pe_w_proj, pe_b_gate, ln_g, ln_b)


def kernel(x_prompt, x_sample, cache_k, cache_v, state_hgrn, page_table, p_prompt, p_sample, ln_g, ln_b, ffn_w_gate, ffn_w_up, ffn_w_down, w_in, b_merge, sb_bias, hgrn_lb, hgrn_norm_g, w_oa, w_ob, w_out, pe_w_proj, pe_w_gate, pe_b_gate):
    depth = ln_g.shape[0]
    alpha = (2.0 * depth) ** 0.25
    batch, seq, _ = x_prompt.shape
    dec_b, dec_seq, _ = x_sample.shape
    n_pool, page = cache_k.shape[1], cache_k.shape[2]

    lb_all = jnp.cumsum(jax.nn.softmax(hgrn_lb.astype(F32), axis=0), axis=0)
    lb_all = lb_all - lb_all[0:1]

    wg = ffn_w_gate.astype(BF16)
    wu = ffn_w_up.astype(BF16)
    wd = ffn_w_down.astype(BF16)
    w_in_b = w_in.astype(BF16)
    w_kv_t = jnp.swapaxes(w_in_b[:, :, COL:3 * COL], 1, 2)
    w_oa_b = w_oa.astype(BF16)
    w_ob_b = w_ob.astype(BF16)
    w_out_b = w_out.astype(BF16)
    pe_wg_b = pe_w_gate.astype(BF16)
    pe_wp_b = pe_w_proj.astype(BF16)
    ln_g4 = ln_g.reshape(depth, 4, 1, D_MODEL)
    ln_b4 = ln_b.reshape(depth, 4, 1, D_MODEL)
    pe_bg3 = pe_b_gate.reshape(depth, 1, D_MODEL)
    cache_kt = jnp.transpose(cache_k, (0, 1, 3, 4, 2)).reshape(depth, n_pool, WIDTH_A, page)
    cache_vt = jnp.transpose(cache_v, (0, 1, 3, 4, 2)).reshape(depth, n_pool, WIDTH_A, page)
    p_prompt3 = p_prompt.reshape(depth, batch * seq, P_DIM)
    p_sample3 = p_sample.reshape(depth, dec_b * dec_seq, P_DIM)
    state_t = jnp.swapaxes(state_hgrn, -1, -2)
    zero_state = jnp.zeros((batch, N_HEADS_B, DV_B, DK_B), F32)

    def layer_tail(x, p3, i, slots, oa, ob):
        x = _merge_ln(x, oa, ob, slots, b_merge, w_oa_b, w_ob_b, w_out_b, ln_g4, ln_b4, i, alpha)
        x = _ffn_ln(x, wg, wu, wd, ln_g4, ln_b4, i, 1, 2, alpha)
        return _embed_ln(x, p3, pe_wg_b, pe_wp_b, pe_bg3, ln_g4, ln_b4, i, alpha)

    yp = x_prompt.reshape(batch * seq, D_MODEL)
    ys = x_sample.reshape(dec_b * dec_seq, D_MODEL)
    kt_all = vt_all = None
    sp, kd, vd, sd = [], [], [], []
    for i in range(depth):
        lb_row = lb_all[i][None, :]
        ng_row = hgrn_norm_g[i][None, :]
        yp = _ffn_ln(yp, wg, wu, wd, ln_g4, ln_b4, i, 0, 0, alpha)
        slots, kt_all, vt_all = _in_proj_prompt(yp, w_in_b, w_kv_t, i, depth, batch, seq, kt_all, vt_all)
        oa = _attn_prompt(slots, kt_all, vt_all, sb_bias[i], i, batch, seq)
        ob, s_t = _hgrn(slots, lb_row, ng_row, zero_state, batch, seq)
        yp = layer_tail(yp, p_prompt3, i, slots, oa, ob)
        sp.append(jnp.swapaxes(s_t, -1, -2))
        ys = _ffn_ln(ys, wg, wu, wd, ln_g4, ln_b4, i, 0, 0, alpha)
        slots, k_new, v_new = _in_proj_rows(ys, w_in_b, i)
        oa = _attn_decode(slots, k_new, v_new, cache_kt, cache_vt, page_table, sb_bias[i], i,
                          dec_b, dec_seq)
        ob, s_t = _hgrn(slots, lb_row, ng_row, state_t[i], dec_b, dec_seq)
        ys = layer_tail(ys, p_sample3, i, slots, oa, ob)
        kd.append(k_new.reshape(dec_b, dec_seq, N_HEADS_A, HEAD_DIM_A))
        vd.append(v_new.reshape(dec_b, dec_seq, N_HEADS_A, HEAD_DIM_A))
        sd.append(jnp.swapaxes(s_t, -1, -2))

    def token_major(t_all):
        t5 = t_all.reshape(depth, batch, N_HEADS_A, HEAD_DIM_A, seq)
        return jnp.transpose(t5, (0, 1, 4, 2, 3))

    return (yp.reshape(batch, seq, D_MODEL), ys.reshape(dec_b, dec_seq, D_MODEL),
            token_major(kt_all), token_major(vt_all), jnp.stack(sp),
            jnp.stack(kd), jnp.stack(vd), jnp.stack(sd))
```
